```python
import jax, jax.numpy as jnp
from jax import lax
import numpy as np

D_MODEL = 1024
BATCH = 4
SEQ = 4096
DEPTH = 4

N_MIXERS = 3
HEAD_DIM = 64
N_Q_HEADS = D_MODEL // HEAD_DIM
N_KV_HEADS = 4
GQA_GROUP = N_Q_HEADS // N_KV_HEADS
QKV_WIDTH = (N_Q_HEADS + 2 * N_KV_HEADS) * HEAD_DIM
ROPE_THETA = 10000.0
SWA_WINDOW = 128
RET_HEADS = 4
RET_KEY_DIM = D_MODEL // RET_HEADS
RET_VAL_DIM = 2 * RET_KEY_DIM
RET_IN_WIDTH = 2 * D_MODEL + 2 * RET_HEADS * RET_VAL_DIM
RET_CHUNK = 128
MOBA_BLOCK = 256
MOBA_TOPK = 3
MOBA_Q_CHUNK = 16
D_FF = ((8 * D_MODEL // 3 + 127) // 128) * 128
CONV_WIDTH = 3
EPS = 1e-6
NEG_INF = -1e30

kernel_name = "hybrid_swa_retnet_moba_convffn"


def rms_norm(x, g):
    xf = x.astype(jnp.float32)
    y = xf * lax.rsqrt(jnp.mean(xf * xf, axis=-1, keepdims=True) + EPS)
    return (y * g.astype(jnp.float32)).astype(x.dtype)


def rope_tables(positions, dim):
    inv = 1.0 / (ROPE_THETA ** (jnp.arange(0, dim, 2, dtype=jnp.float32) / dim))
    ang = positions.astype(jnp.float32)[..., None] * inv
    return jnp.cos(ang)[:, :, None, :], jnp.sin(ang)[:, :, None, :]


def apply_rope(x, cos, sin):
    x1, x2 = jnp.split(x.astype(jnp.float32), 2, axis=-1)
    return jnp.concatenate([x1 * cos - x2 * sin, x2 * cos + x1 * sin], axis=-1).astype(x.dtype)


def swa_sink_attention(h, w_in, w_out, sinks, positions):
    B, S, _ = h.shape
    W = SWA_WINDOW
    nb = S // W
    proj = h @ w_in
    q, k, v = jnp.split(proj, [N_Q_HEADS * HEAD_DIM, (N_Q_HEADS + N_KV_HEADS) * HEAD_DIM], axis=-1)
    q = q.reshape(B, S, N_KV_HEADS, GQA_GROUP, HEAD_DIM)
    k = k.reshape(B, S, N_KV_HEADS, HEAD_DIM)
    v = v.reshape(B, S, N_KV_HEADS, HEAD_DIM)
    cos, sin = rope_tables(positions, HEAD_DIM)
    q = apply_rope(q, cos[:, :, :, None, :], sin[:, :, :, None, :])
    k = apply_rope(k, cos, sin)
    qb = q.reshape(B, nb, W, N_KV_HEADS, GQA_GROUP, HEAD_DIM)
    kpad = jnp.pad(k, ((0, 0), (W, 0), (0, 0), (0, 0)))
    vpad = jnp.pad(v, ((0, 0), (W, 0), (0, 0), (0, 0)))
    kb = jnp.concatenate([kpad[:, :-W].reshape(B, nb, W, N_KV_HEADS, HEAD_DIM),
                          k.reshape(B, nb, W, N_KV_HEADS, HEAD_DIM)], axis=2)
    vb = jnp.concatenate([vpad[:, :-W].reshape(B, nb, W, N_KV_HEADS, HEAD_DIM),
                          v.reshape(B, nb, W, N_KV_HEADS, HEAD_DIM)], axis=2)
    s = jnp.einsum('bnqkgd,bnjkd->bnkgqj', qb, kb).astype(jnp.float32) * (HEAD_DIM ** -0.5)
    qi = jnp.arange(W)[:, None]
    kj = jnp.arange(2 * W)[None, :]
    rel = qi + W - kj
    band = (rel >= 0) & (rel < W)
    blk = jnp.arange(nb)[:, None, None]
    mask = band[None] & ((blk > 0) | (kj[None] >= W))
    s = jnp.where(mask[None, :, None, None], s, NEG_INF)
    sink = jnp.broadcast_to(sinks.astype(jnp.float32).reshape(1, 1, N_KV_HEADS, GQA_GROUP, 1, 1),
                            s.shape[:-1] + (1,))
    p = jax.nn.softmax(jnp.concatenate([s, sink], axis=-1), axis=-1)[..., :-1]
    o = jnp.einsum('bnkgqj,bnjkd->bnqkgd', p.astype(v.dtype), vb)
    return o.reshape(B, S, D_MODEL) @ w_out


def retention(h, w_in, w_out, gn_g, positions):
    B, S, _ = h.shape
    H, dk, dv, C = RET_HEADS, RET_KEY_DIM, RET_VAL_DIM, RET_CHUNK
    nc = S // C
    proj = h @ w_in
    q, k, v, g = jnp.split(proj, [D_MODEL, 2 * D_MODEL, 2 * D_MODEL + H * dv], axis=-1)
    cos, sin = rope_tables(positions, dk)
    q = apply_rope(q.reshape(B, S, H, dk), cos, sin)
    k = apply_rope(k.reshape(B, S, H, dk), cos, sin) * (dk ** -0.5)
    v = v.reshape(B, S, H, dv)
    log_gamma = jnp.log(1.0 - 2.0 ** (-5.0 - jnp.arange(H, dtype=jnp.float32)))
    idx = jnp.arange(C, dtype=jnp.float32)
    diff = idx[:, None] - idx[None, :]
    decay_in = jnp.where(diff[None] >= 0, jnp.exp(jnp.maximum(diff, 0.0)[None] * log_gamma[:, None, None]), 0.0)
    q_decay = jnp.exp((idx + 1.0)[None, :] * log_gamma[:, None])
    k_decay = jnp.exp((C - 1.0 - idx)[None, :] * log_gamma[:, None])
    chunk_decay = jnp.exp(C * log_gamma)
    qc = q.reshape(B, nc, C, H, dk)
    kc = k.reshape(B, nc, C, H, dk)
    vc = v.reshape(B, nc, C, H, dv)
    sc = jnp.einsum('bnihd,bnjhd->bnhij', qc, kc) * decay_in[None, None]
    inner = jnp.einsum('bnhij,bnjhe->bnihe', sc, vc)

    def step(R, xs):
        q_n, k_n, v_n = xs
        cross = jnp.einsum('bihd,hi,bhde->bihe', q_n, q_decay, R)
        R = chunk_decay[None, :, None, None] * R + jnp.einsum('bjhd,hj,bjhe->bhde', k_n, k_decay, v_n)
        return R, cross

    R0 = jnp.zeros((B, H, dk, dv), jnp.float32)
    _, cross = lax.scan(step, R0, (jnp.moveaxis(qc, 1, 0), jnp.moveaxis(kc, 1, 0), jnp.moveaxis(vc, 1, 0)))
    o = (inner + jnp.moveaxis(cross, 0, 1)).astype(jnp.float32).reshape(B, S, H, dv)
    mu = jnp.mean(o, axis=-1, keepdims=True)
    var = jnp.mean(jnp.square(o - mu), axis=-1, keepdims=True)
    o = (o - mu) * lax.rsqrt(var + EPS) * gn_g.astype(jnp.float32).reshape(H, dv)
    y = jax.nn.silu(g.astype(jnp.float32)) * o.reshape(B, S, H * dv)
    return y.astype(h.dtype) @ w_out


def moba_attention(h, w_in, w_out, positions):
    B, S, _ = h.shape
    BS, QC = MOBA_BLOCK, MOBA_Q_CHUNK
    nb = -(-S // BS)
    pad = nb * BS - S
    nq = S // QC
    topk = min(MOBA_TOPK, nb)
    scale = HEAD_DIM ** -0.5
    proj = h @ w_in
    q, k, v = jnp.split(proj, [N_Q_HEADS * HEAD_DIM, (N_Q_HEADS + N_KV_HEADS) * HEAD_DIM], axis=-1)
    cos, sin = rope_tables(positions, HEAD_DIM)
    q = apply_rope(q.reshape(B, S, N_Q_HEADS, HEAD_DIM), cos, sin)
    k = apply_rope(k.reshape(B, S, N_KV_HEADS, HEAD_DIM), cos, sin)
    v = v.reshape(B, S, N_KV_HEADS, HEAD_DIM)
    k = jnp.repeat(k, GQA_GROUP, axis=2)
    v = jnp.repeat(v, GQA_GROUP, axis=2)
    kp = jnp.pad(k, ((0, 0), (0, pad), (0, 0), (0, 0))).transpose(0, 2, 1, 3).reshape(B, N_Q_HEADS, nb, BS, HEAD_DIM)
    vp = jnp.pad(v, ((0, 0), (0, pad), (0, 0), (0, 0))).transpose(0, 2, 1, 3).reshape(B, N_Q_HEADS, nb, BS, HEAD_DIM)
    kmean = jnp.mean(kp.astype(jnp.float32), axis=3).astype(k.dtype)
    qh = q.transpose(0, 2, 1, 3)
    gather = jax.vmap(jax.vmap(lambda blocks, sel: blocks[sel]))

    def chunk(ci):
        start = ci * QC
        own = start // BS
        t = start + jnp.arange(QC)
        qc = lax.dynamic_slice_in_dim(qh, start, QC, axis=2)
        gate = jnp.einsum('bhqd,bhnd->bhqn', qc, kmean).astype(jnp.float32)
        gate = jnp.where(jnp.arange(nb) < own, gate, NEG_INF)
        _, sel = lax.top_k(gate, topk)
        sel_valid = jnp.arange(topk) < jnp.minimum(own, topk)
        ks = gather(kp, sel)
        vs = gather(vp, sel)
        s_sel = jnp.einsum('bhqd,bhqnjd->bhqnj', qc, ks).astype(jnp.float32) * scale
        s_sel = jnp.where(sel_valid[:, None], s_sel, NEG_INF).reshape(B, N_Q_HEADS, QC, topk * BS)
        k_own = lax.dynamic_index_in_dim(kp, own, axis=2, keepdims=False)
        v_own = lax.dynamic_index_in_dim(vp, own, axis=2, keepdims=False)
        s_own = jnp.einsum('bhqd,bhjd->bhqj', qc, k_own).astype(jnp.float32) * scale
        kpos = own * BS + jnp.arange(BS)
        s_own = jnp.where(kpos[None, :] <= t[:, None], s_own, NEG_INF)
        p = jax.nn.softmax(jnp.concatenate([s_sel, s_own], axis=-1), axis=-1).astype(v.dtype)
        p_sel = p[..., :topk * BS].reshape(B, N_Q_HEADS, QC, topk, BS)
        p_own = p[..., topk * BS:]
        return (jnp.einsum('bhqnj,bhqnjd->bhqd', p_sel, vs)
                + jnp.einsum('bhqj,bhjd->bhqd', p_own, v_own))

    o = lax.map(chunk, jnp.arange(nq))
    o = o.transpose(1, 0, 3, 2, 4).reshape(B, S, D_MODEL)
    return o @ w_out


def conv_glu_ffn(h, w_a, w_b, conv_w, conv_b, w_down):
    a = h @ w_a
    a = lax.conv_general_dilated(a, conv_w.reshape(CONV_WIDTH, 1, D_FF).astype(a.dtype),
                                 window_strides=(1,), padding=[(CONV_WIDTH - 1, 0)],
                                 dimension_numbers=('NWC', 'WIO', 'NWC'),
                                 feature_group_count=D_FF)
    return (jax.nn.silu(a + conv_b) * (h @ w_b)) @ w_down


def setup_inputs(seed: int = 0) -> dict:
    key = jax.random.key(seed)
    keys = iter(jax.random.split(key, 16 * DEPTH + 4))

    def dense(fan_in, fan_out):
        return jax.random.normal(next(keys), (fan_in, fan_out), jnp.float32) * (fan_in ** -0.5)

    def gain(n):
        return 1.0 + 0.02 * jax.random.normal(next(keys), (n,), jnp.float32)

    d = {}
    d["x"] = jax.random.normal(next(keys), (BATCH, SEQ, D_MODEL), jnp.float32)
    d["positions"] = jnp.broadcast_to(jnp.arange(SEQ, dtype=jnp.int32)[None, :], (BATCH, SEQ))
    for i in range(DEPTH):
        m = i % N_MIXERS
        p = "l%d_" % i
        d[p + "attn_norm"] = gain(D_MODEL)
        if m == 0:
            d[p + "w_in"] = dense(D_MODEL, QKV_WIDTH)
            d[p + "w_out"] = dense(D_MODEL, D_MODEL)
            d[p + "sinks"] = 0.5 * jax.random.normal(next(keys), (N_Q_HEADS,), jnp.float32)
        elif m == 1:
            d[p + "w_in"] = dense(D_MODEL, RET_IN_WIDTH)
            d[p + "w_out"] = dense(RET_HEADS * RET_VAL_DIM, D_MODEL)
            d[p + "gn_g"] = gain(RET_HEADS * RET_VAL_DIM)
        else:
            d[p + "w_in"] = dense(D_MODEL, QKV_WIDTH)
            d[p + "w_out"] = dense(D_MODEL, D_MODEL)
        d[p + "ffn_norm"] = gain(D_MODEL)
        d[p + "w_a"] = dense(D_MODEL, D_FF)
        d[p + "w_b"] = dense(D_MODEL, D_FF)
        d[p + "conv_w"] = jax.random.normal(next(keys), (CONV_WIDTH, D_FF), jnp.float32) * (CONV_WIDTH ** -0.5)
        d[p + "conv_b"] = 0.02 * jax.random.normal(next(keys), (D_FF,), jnp.float32)
        d[p + "w_down"] = dense(D_FF, D_MODEL)
    d["final_norm"] = gain(D_MODEL)
    return d


def reference(x, positions,
              l0_attn_norm, l0_w_in, l0_w_out, l0_sinks,
              l0_ffn_norm, l0_w_a, l0_w_b, l0_conv_w, l0_conv_b, l0_w_down,
              l1_attn_norm, l1_w_in, l1_w_out, l1_gn_g,
              l1_ffn_norm, l1_w_a, l1_w_b, l1_conv_w, l1_conv_b, l1_w_down,
              l2_attn_norm, l2_w_in, l2_w_out,
              l2_ffn_norm, l2_w_a, l2_w_b, l2_conv_w, l2_conv_b, l2_w_down,
              l3_attn_norm, l3_w_in, l3_w_out, l3_sinks,
              l3_ffn_norm, l3_w_a, l3_w_b, l3_conv_w, l3_conv_b, l3_w_down,
              final_norm):
    mixer_norms = [l0_attn_norm, l1_attn_norm, l2_attn_norm, l3_attn_norm]
    mixer_args = [(l0_w_in, l0_w_out, l0_sinks),
                  (l1_w_in, l1_w_out, l1_gn_g),
                  (l2_w_in, l2_w_out),
                  (l3_w_in, l3_w_out, l3_sinks)]
    ffn_norms = [l0_ffn_norm, l1_ffn_norm, l2_ffn_norm, l3_ffn_norm]
    ffn_args = [(l0_w_a, l0_w_b, l0_conv_w, l0_conv_b, l0_w_down),
                (l1_w_a, l1_w_b, l1_conv_w, l1_conv_b, l1_w_down),
                (l2_w_a, l2_w_b, l2_conv_w, l2_conv_b, l2_w_down),
                (l3_w_a, l3_w_b, l3_conv_w, l3_conv_b, l3_w_down)]
    for i in range(DEPTH):
        hn = rms_norm(x, mixer_norms[i])
        m = i % N_MIXERS
        if m == 0:
            y = swa_sink_attention(hn, *mixer_args[i], positions)
        elif m == 1:
            y = retention(hn, *mixer_args[i], positions)
        else:
            y = moba_attention(hn, *mixer_args[i], positions)
        x = x + y.astype(x.dtype)
        x = x + conv_glu_ffn(rms_norm(x, ffn_norms[i]), *ffn_args[i]).astype(x.dtype)
    return rms_norm(x, final_norm)
```

```python
import functools

import jax
import jax.numpy as jnp
from jax import lax
from jax.experimental import pallas as pl
from jax.experimental.pallas import tpu as pltpu

D_MODEL = 1024
BATCH = 4
SEQ = 4096
DEPTH = 4
N_MIXERS = 3
HEAD_DIM = 64
N_Q_HEADS = D_MODEL // HEAD_DIM
N_KV_HEADS = 4
GQA_GROUP = N_Q_HEADS // N_KV_HEADS
QKV_WIDTH = (N_Q_HEADS + 2 * N_KV_HEADS) * HEAD_DIM
ROPE_THETA = 10000.0
SWA_WINDOW = 128
RET_HEADS = 4
RET_KEY_DIM = D_MODEL // RET_HEADS
RET_VAL_DIM = 2 * RET_KEY_DIM
RET_IN_WIDTH = 2 * D_MODEL + 2 * RET_HEADS * RET_VAL_DIM
RET_CHUNK = 128
MOBA_BLOCK = 256
MOBA_TOPK = 3
D_FF = ((8 * D_MODEL // 3 + 127) // 128) * 128
CONV_WIDTH = 3
EPS = 1e-6
NEG_INF = -1e30

LANES = 128
BF16_ROWS = 16
VMEM_LIMIT_BYTES = 56 * 1024 * 1024

TOKENS = BATCH * SEQ
PROJ_TM = 512
ATT_TQ = 512
RET_TQ = 512
FFN_TM = 512
FFN_CHUNK = 256
FFN_HALO = BF16_ROWS

f32 = jnp.float32
bf16 = jnp.bfloat16


def _params(*semantics):
    return pltpu.CompilerParams(dimension_semantics=semantics, vmem_limit_bytes=VMEM_LIMIT_BYTES)


def _rms_norm_bf16(x, gain):
    ms = jnp.mean(x * x, axis=-1, keepdims=True)
    return (x * lax.rsqrt(ms + EPS) * gain).astype(bf16)


def _silu(x):
    return x / (1.0 + jnp.exp(-x))


def _attn_proj_kernel(x_ref, g_ref, w_ref, cos_ref, sin_ref, q_ref, k_ref, v_ref):
    h = _rms_norm_bf16(x_ref[...], g_ref[...])
    proj = jnp.dot(h, w_ref[...], preferred_element_type=f32)
    cosf = cos_ref[...]
    sins = sin_ref[...]
    lane = lax.broadcasted_iota(jnp.int32, (PROJ_TM, LANES), 1)
    first_half = (lane % HEAD_DIM) < (HEAD_DIM // 2)
    n_rope = (N_Q_HEADS + N_KV_HEADS) * HEAD_DIM // LANES
    n_q = N_Q_HEADS * HEAD_DIM // LANES
    for c in range(n_rope):
        xc = proj[:, c * LANES:(c + 1) * LANES]
        partner = jnp.where(first_half, pltpu.roll(xc, LANES - HEAD_DIM // 2, 1), pltpu.roll(xc, HEAD_DIM // 2, 1))
        r = xc * cosf + partner * sins
        if c < n_q:
            r = (r * (HEAD_DIM ** -0.5)).astype(bf16)
            q_ref[2 * c] = r[:, :HEAD_DIM]
            q_ref[2 * c + 1] = r[:, HEAD_DIM:]
        else:
            r = r.astype(bf16)
            k_ref[2 * (c - n_q)] = r[:, :HEAD_DIM]
            k_ref[2 * (c - n_q) + 1] = r[:, HEAD_DIM:]
    v0 = (N_Q_HEADS + N_KV_HEADS) * HEAD_DIM
    for j in range(N_KV_HEADS):
        v_ref[j] = proj[:, v0 + j * HEAD_DIM:v0 + (j + 1) * HEAD_DIM].astype(bf16)


def _attn_proj(x, gain, w_bf16, cosf, sins):
    nt = SEQ // PROJ_TM
    return pl.pallas_call(
        _attn_proj_kernel,
        grid=(BATCH, nt),
        in_specs=[
            pl.BlockSpec((None, PROJ_TM, D_MODEL), lambda b, t: (b, t, 0)),
            pl.BlockSpec((1, D_MODEL), lambda b, t: (0, 0)),
            pl.BlockSpec((D_MODEL, QKV_WIDTH), lambda b, t: (0, 0)),
            pl.BlockSpec((None, PROJ_TM, LANES), lambda b, t: (b, t, 0)),
            pl.BlockSpec((None, PROJ_TM, LANES), lambda b, t: (b, t, 0)),
        ],
        out_specs=[
            pl.BlockSpec((None, N_Q_HEADS, PROJ_TM, HEAD_DIM), lambda b, t: (b, 0, t, 0)),
            pl.BlockSpec((None, N_KV_HEADS, PROJ_TM, HEAD_DIM), lambda b, t: (b, 0, t, 0)),
            pl.BlockSpec((None, N_KV_HEADS, PROJ_TM, HEAD_DIM), lambda b, t: (b, 0, t, 0)),
        ],
        out_shape=[
            jax.ShapeDtypeStruct((BATCH, N_Q_HEADS, SEQ, HEAD_DIM), bf16),
            jax.ShapeDtypeStruct((BATCH, N_KV_HEADS, SEQ, HEAD_DIM), bf16),
            jax.ShapeDtypeStruct((BATCH, N_KV_HEADS, SEQ, HEAD_DIM), bf16),
        ],
        compiler_params=_params("parallel", "parallel"),
        name="attn_proj",
    )(x, gain.reshape(1, D_MODEL), w_bf16, cosf, sins)


def _swa_kernel(sinks_ref, q_ref, k_ref, v_ref, o_ref):
    g = pl.program_id(1)
    t = pl.program_id(2)
    W = SWA_WINDOW
    rows = GQA_GROUP * W
    row = lax.broadcasted_iota(jnp.int32, (rows, 1), 0)
    sink = jnp.zeros((rows, 1), f32)
    for j in range(GQA_GROUP):
        sink = jnp.where(row // W == j, sinks_ref[g * GQA_GROUP + j], sink)
    qi = lax.broadcasted_iota(jnp.int32, (rows, 2 * W), 0) % W
    col = lax.broadcasted_iota(jnp.int32, (rows, 2 * W), 1)
    for c in range(ATT_TQ // W):
        start = t * ATT_TQ + c * W
        kstart = pl.multiple_of(jnp.maximum(start - W, 0), W)
        q4 = q_ref[:, c * W:(c + 1) * W, :].reshape(rows, HEAD_DIM)
        kk = k_ref[pl.ds(kstart, 2 * W), :]
        vv = v_ref[pl.ds(kstart, 2 * W), :]
        s = lax.dot_general(q4, kk, (((1,), (1,)), ((), ())), preferred_element_type=f32)
        rel = (start + qi) - (kstart + col)
        s = jnp.where((rel >= 0) & (rel < W), s, NEG_INF)
        m = jnp.maximum(jnp.max(s, axis=1, keepdims=True), sink)
        p = jnp.exp(s - m)
        denom = jnp.sum(p, axis=1, keepdims=True) + jnp.exp(sink - m)
        o = jnp.dot(p.astype(bf16), vv, preferred_element_type=f32) / denom
        for j in range(GQA_GROUP):
            o_ref[c * W:(c + 1) * W, j * HEAD_DIM:(j + 1) * HEAD_DIM] = o[j * W:(j + 1) * W].astype(bf16)


def _swa(q, k, v, sinks):
    nt = SEQ // ATT_TQ
    gw = GQA_GROUP * HEAD_DIM
    return pl.pallas_call(
        _swa_kernel,
        grid=(BATCH, N_KV_HEADS, nt),
        in_specs=[
            pl.BlockSpec(memory_space=pltpu.SMEM),
            pl.BlockSpec((None, GQA_GROUP, ATT_TQ, HEAD_DIM), lambda b, g, t: (b, g, t, 0)),
            pl.BlockSpec((None, None, SEQ, HEAD_DIM), lambda b, g, t: (b, g, 0, 0)),
            pl.BlockSpec((None, None, SEQ, HEAD_DIM), lambda b, g, t: (b, g, 0, 0)),
        ],
        out_specs=pl.BlockSpec((None, ATT_TQ, gw), lambda b, g, t: (b, t, g)),
        out_shape=jax.ShapeDtypeStruct((BATCH, SEQ, D_MODEL), bf16),
        compiler_params=_params("parallel", "parallel", "parallel"),
        name="swa_core",
    )(sinks, q, k, v)


def _moba_kernel(q_ref, k_ref, v_ref, o_ref, kmean_ref):
    i = pl.program_id(2)
    BS = MOBA_BLOCK
    nb = SEQ // BS
    rows = GQA_GROUP * BS

    @pl.when(i == 0)
    def _():
        kmean_ref[...] = jnp.zeros_like(kmean_ref)
        for n in range(nb):
            kb = k_ref[n * BS:(n + 1) * BS, :].astype(f32)
            kmean_ref[n:n + 1, :] = jnp.mean(kb, axis=0, keepdims=True)

    q4 = q_ref[...].reshape(rows, HEAD_DIM)
    gate = lax.dot_general(q4, kmean_ref[...].astype(bf16), (((1,), (1,)), ((), ())),
                           preferred_element_type=f32)
    lane = lax.broadcasted_iota(jnp.int32, (rows, LANES), 1)
    past = lane < i
    gm = jnp.where(past, gate, NEG_INF)
    sel = jnp.zeros((rows, LANES), jnp.int32)
    for _ in range(MOBA_TOPK):
        mx = jnp.max(gm, axis=1, keepdims=True)
        cand = (gm == mx) & past & (sel == 0)
        idx = jnp.min(jnp.where(cand, lane, LANES), axis=1, keepdims=True)
        pick = lane == idx
        sel = jnp.where(pick, 1, sel)
        gm = jnp.where(pick, NEG_INF, gm)
    bias = jnp.where((sel == 0) & (lane < nb), NEG_INF, 0.0).astype(bf16)

    k_own = k_ref[pl.ds(pl.multiple_of(i * BS, BS), BS), :]
    v_own = v_ref[pl.ds(pl.multiple_of(i * BS, BS), BS), :]
    s = lax.dot_general(q4, k_own, (((1,), (1,)), ((), ())), preferred_element_type=f32)
    qi = lax.broadcasted_iota(jnp.int32, (rows, BS), 0) % BS
    kj = lax.broadcasted_iota(jnp.int32, (rows, BS), 1)
    s = jnp.where(kj <= qi, s, NEG_INF)
    m0 = jnp.max(s, axis=1, keepdims=True)
    p = jnp.exp(s - m0)
    l0 = jnp.sum(p, axis=1, keepdims=True)
    acc0 = jnp.dot(p.astype(bf16), v_own, preferred_element_type=f32)
    blk = lax.broadcasted_iota(jnp.int32, (LANES, BS), 0)

    def body(n, carry):
        m, l, acc = carry
        k_n = k_ref[pl.ds(pl.multiple_of(n * BS, BS), BS), :]
        v_n = v_ref[pl.ds(pl.multiple_of(n * BS, BS), BS), :]
        e_n = jnp.where(blk == n, 1.0, 0.0).astype(bf16)
        s = (lax.dot_general(q4, k_n, (((1,), (1,)), ((), ())), preferred_element_type=f32)
             + jnp.dot(bias, e_n, preferred_element_type=f32))
        m_new = jnp.maximum(m, jnp.max(s, axis=1, keepdims=True))
        alpha = jnp.exp(m - m_new)
        p = jnp.exp(s - m_new)
        l = alpha * l + jnp.sum(p, axis=1, keepdims=True)
        acc = alpha * acc + jnp.dot(p.astype(bf16), v_n, preferred_element_type=f32)
        return m_new, l, acc

    _, l, acc = lax.fori_loop(0, i, body, (m0, l0, acc0))
    o = acc / l
    for j in range(GQA_GROUP):
        o_ref[:, j * HEAD_DIM:(j + 1) * HEAD_DIM] = o[j * BS:(j + 1) * BS].astype(bf16)


def _moba(q, k, v):
    nb = SEQ // MOBA_BLOCK
    gw = GQA_GROUP * HEAD_DIM
    return pl.pallas_call(
        _moba_kernel,
        grid=(BATCH, N_KV_HEADS, nb),
        in_specs=[
            pl.BlockSpec((None, GQA_GROUP, MOBA_BLOCK, HEAD_DIM), lambda b, g, i: (b, g, i, 0)),
            pl.BlockSpec((None, None, SEQ, HEAD_DIM), lambda b, g, i: (b, g, 0, 0)),
            pl.BlockSpec((None, None, SEQ, HEAD_DIM), lambda b, g, i: (b, g, 0, 0)),
        ],
        out_specs=pl.BlockSpec((None, MOBA_BLOCK, gw), lambda b, g, i: (b, i, g)),
        out_shape=jax.ShapeDtypeStruct((BATCH, SEQ, D_MODEL), bf16),
        scratch_shapes=[pltpu.VMEM((LANES, HEAD_DIM), f32)],
        compiler_params=_params("parallel", "parallel", "arbitrary"),
        name="moba_core",
    )(q, k, v)


RET_TN = 1024


def _ret_proj_kernel(x_ref, g_ref, w_ref, cos_ref, sin_ref, o_ref, h_ref):
    j = pl.program_id(1)

    @pl.when(j == 0)
    def _():
        h_ref[...] = _rms_norm_bf16(x_ref[...], g_ref[...])

    proj = jnp.dot(h_ref[...], w_ref[...], preferred_element_type=f32)

    @pl.when(j < 2)
    def _():
        cos = cos_ref[...]
        sin = sin_ref[...]
        scale = jnp.where(j == 1, RET_KEY_DIM ** -0.5, 1.0)
        half = RET_KEY_DIM // 2
        for hh in range(RET_TN // RET_KEY_DIM):
            x1 = proj[:, hh * RET_KEY_DIM:hh * RET_KEY_DIM + half]
            x2 = proj[:, hh * RET_KEY_DIM + half:(hh + 1) * RET_KEY_DIM]
            o_ref[:, hh * RET_KEY_DIM:hh * RET_KEY_DIM + half] = ((x1 * cos - x2 * sin) * scale).astype(bf16)
            o_ref[:, hh * RET_KEY_DIM + half:(hh + 1) * RET_KEY_DIM] = ((x2 * cos + x1 * sin) * scale).astype(bf16)

    @pl.when(j >= 2)
    def _():
        o_ref[...] = proj.astype(bf16)


def _ret_proj(x2d, gain, w_bf16, cos, sin):
    nt = TOKENS // PROJ_TM
    nj = RET_IN_WIDTH // RET_TN
    return pl.pallas_call(
        _ret_proj_kernel,
        grid=(nt, nj),
        in_specs=[
            pl.BlockSpec((PROJ_TM, D_MODEL), lambda t, j: (t, 0)),
            pl.BlockSpec((1, D_MODEL), lambda t, j: (0, 0)),
            pl.BlockSpec((D_MODEL, RET_TN), lambda t, j: (0, j)),
            pl.BlockSpec((PROJ_TM, LANES), lambda t, j: (t, 0)),
            pl.BlockSpec((PROJ_TM, LANES), lambda t, j: (t, 0)),
        ],
        out_specs=pl.BlockSpec((PROJ_TM, RET_TN), lambda t, j: (t, j)),
        out_shape=jax.ShapeDtypeStruct((TOKENS, RET_IN_WIDTH), bf16),
        scratch_shapes=[pltpu.VMEM((PROJ_TM, D_MODEL), bf16)],
        compiler_params=_params("parallel", "arbitrary"),
        name="ret_proj",
    )(x2d, gain.reshape(1, D_MODEL), w_bf16, cos, sin)


def _ret_core_kernel(cd_ref, q_ref, k_ref, v_ref, g_ref, din_ref, qdec_ref, kdec_ref, gn_ref, y_ref, r_ref):
    h = pl.program_id(1)
    t = pl.program_id(2)
    C = RET_CHUNK

    @pl.when(t == 0)
    def _():
        r_ref[...] = jnp.zeros_like(r_ref)

    decay_in = din_ref[...]
    qdec = qdec_ref[...]
    kdec = kdec_ref[...]
    cd = cd_ref[h]
    gn = gn_ref[...]
    for c in range(RET_TQ // C):
        sl = slice(c * C, (c + 1) * C)
        qn = q_ref[sl, :]
        kn = k_ref[sl, :]
        vn = v_ref[sl, :]
        sc = lax.dot_general(qn, kn, (((1,), (1,)), ((), ())), preferred_element_type=f32) * decay_in
        inner = jnp.dot(sc.astype(bf16), vn, preferred_element_type=f32)
        r_old = r_ref[...]
        qd = (qn.astype(f32) * qdec).astype(bf16)
        cross = jnp.dot(qd, r_old.astype(bf16), preferred_element_type=f32)
        kdt = (kn.astype(f32) * kdec).T.astype(bf16)
        r_ref[...] = cd * r_old + jnp.dot(kdt, vn, preferred_element_type=f32)
        o = inner + cross
        mu = jnp.mean(o, axis=-1, keepdims=True)
        d = o - mu
        var = jnp.mean(d * d, axis=-1, keepdims=True)
        on = d * lax.rsqrt(var + EPS) * gn
        y_ref[sl, :] = (_silu(g_ref[sl, :].astype(f32)) * on).astype(bf16)


def _ret_core(proj3d, gn_g, decay_in, q_decay, k_decay, chunk_decay):
    nt = SEQ // RET_TQ
    H, dk, dv, C = RET_HEADS, RET_KEY_DIM, RET_VAL_DIM, RET_CHUNK
    v_blk0 = 2 * D_MODEL // dv
    g_blk0 = (2 * D_MODEL + H * dv) // dv
    return pl.pallas_call(
        _ret_core_kernel,
        grid=(BATCH, H, nt),
        in_specs=[
            pl.BlockSpec(memory_space=pltpu.SMEM),
            pl.BlockSpec((None, RET_TQ, dk), lambda b, h, t: (b, t, h)),
            pl.BlockSpec((None, RET_TQ, dk), lambda b, h, t: (b, t, H + h)),
            pl.BlockSpec((None, RET_TQ, dv), lambda b, h, t: (b, t, v_blk0 + h)),
            pl.BlockSpec((None, RET_TQ, dv), lambda b, h, t: (b, t, g_blk0 + h)),
            pl.BlockSpec((None, C, C), lambda b, h, t: (h, 0, 0)),
            pl.BlockSpec((None, C, 1), lambda b, h, t: (h, 0, 0)),
            pl.BlockSpec((None, C, 1), lambda b, h, t: (h, 0, 0)),
            pl.BlockSpec((1, dv), lambda b, h, t: (0, h)),
        ],
        out_specs=pl.BlockSpec((None, RET_TQ, dv), lambda b, h, t: (b, t, h)),
        out_shape=jax.ShapeDtypeStruct((BATCH, SEQ, H * dv), bf16),
        scratch_shapes=[pltpu.VMEM((dk, dv), f32)],
        compiler_params=_params("parallel", "parallel", "arbitrary"),
        name="ret_core",
    )(chunk_decay, proj3d, proj3d, proj3d, proj3d, decay_in, q_decay, k_decay, gn_g.reshape(1, H * dv))


def _out_proj_kernel(a_ref, w_ref, x_ref, o_ref):
    o_ref[...] = x_ref[...] + jnp.dot(a_ref[...], w_ref[...], preferred_element_type=f32)


def _out_proj(a2d, w_bf16, x2d):
    kdim = a2d.shape[1]
    nt = TOKENS // PROJ_TM
    return pl.pallas_call(
        _out_proj_kernel,
        grid=(nt,),
        in_specs=[
            pl.BlockSpec((PROJ_TM, kdim), lambda t: (t, 0)),
            pl.BlockSpec((kdim, D_MODEL), lambda t: (0, 0)),
            pl.BlockSpec((PROJ_TM, D_MODEL), lambda t: (t, 0)),
        ],
        out_specs=pl.BlockSpec((PROJ_TM, D_MODEL), lambda t: (t, 0)),
        out_shape=jax.ShapeDtypeStruct((TOKENS, D_MODEL), f32),
        compiler_params=_params("parallel"),
        name="out_proj",
    )(a2d, w_bf16, x2d)


def _ffn_kernel(x_ref, xh_ref, g_ref, wa_ref, wb_ref, cw_ref, cb_ref, wd_ref, o_ref):
    t = pl.program_id(0)
    xt = x_ref[...]
    gain = g_ref[...]
    h_tile = _rms_norm_bf16(xt, gain)
    h_halo = _rms_norm_bf16(xh_ref[...], gain)
    h_full = jnp.concatenate([h_halo, h_tile], axis=0)
    halo_valid = jnp.where((t * FFN_TM) % SEQ == 0, 0.0, 1.0)
    row = lax.broadcasted_iota(jnp.int32, (FFN_HALO + FFN_TM, 1), 0)
    keep = jnp.where(row < FFN_HALO, halo_valid, 1.0)
    acc = xt
    for c in range(D_FF // FFN_CHUNK):
        cs = slice(c * FFN_CHUNK, (c + 1) * FFN_CHUNK)
        a = jnp.dot(h_full, wa_ref[:, cs], preferred_element_type=f32) * keep
        b = jnp.dot(h_tile, wb_ref[:, cs], preferred_element_type=f32)
        a1 = pltpu.roll(a, 1, 0)[FFN_HALO:]
        a2 = pltpu.roll(a, 2, 0)[FFN_HALO:]
        cw = cw_ref[:, cs]
        conv = cw[0:1] * a2 + cw[1:2] * a1 + cw[2:3] * a[FFN_HALO:] + cb_ref[:, cs]
        hid = (_silu(conv) * b).astype(bf16)
        acc = acc + jnp.dot(hid, wd_ref[cs, :], preferred_element_type=f32)
    o_ref[...] = acc


def _ffn(x2d, gain, wa, wb, conv_w, conv_b, wd):
    nt = TOKENS // FFN_TM
    halo_blocks = FFN_TM // FFN_HALO
    resident = dict(pipeline_mode=pl.Buffered(1))
    return pl.pallas_call(
        _ffn_kernel,
        grid=(nt,),
        in_specs=[
            pl.BlockSpec((FFN_TM, D_MODEL), lambda t: (t, 0)),
            pl.BlockSpec((FFN_HALO, D_MODEL), lambda t: (jnp.maximum(t * halo_blocks - 1, 0), 0)),
            pl.BlockSpec((1, D_MODEL), lambda t: (0, 0)),
            pl.BlockSpec((D_MODEL, D_FF), lambda t: (0, 0), **resident),
            pl.BlockSpec((D_MODEL, D_FF), lambda t: (0, 0), **resident),
            pl.BlockSpec((CONV_WIDTH, D_FF), lambda t: (0, 0)),
            pl.BlockSpec((1, D_FF), lambda t: (0, 0)),
            pl.BlockSpec((D_FF, D_MODEL), lambda t: (0, 0), **resident),
        ],
        out_specs=pl.BlockSpec((FFN_TM, D_MODEL), lambda t: (t, 0)),
        out_shape=jax.ShapeDtypeStruct((TOKENS, D_MODEL), f32),
        compiler_params=_params("parallel"),
        name="conv_ffn",
    )(x2d, x2d, gain.reshape(1, D_MODEL), wa, wb, conv_w, conv_b.reshape(1, D_FF), wd)


def _final_norm_kernel(x_ref, g_ref, o_ref):
    x = x_ref[...]
    ms = jnp.mean(x * x, axis=-1, keepdims=True)
    o_ref[...] = x * lax.rsqrt(ms + EPS) * g_ref[...]


def _final_norm(x2d, gain):
    nt = TOKENS // PROJ_TM
    return pl.pallas_call(
        _final_norm_kernel,
        grid=(nt,),
        in_specs=[
            pl.BlockSpec((PROJ_TM, D_MODEL), lambda t: (t, 0)),
            pl.BlockSpec((1, D_MODEL), lambda t: (0, 0)),
        ],
        out_specs=pl.BlockSpec((PROJ_TM, D_MODEL), lambda t: (t, 0)),
        out_shape=jax.ShapeDtypeStruct((TOKENS, D_MODEL), f32),
        compiler_params=_params("parallel"),
        name="final_norm",
    )(x2d, gain.reshape(1, D_MODEL))


def _rope_angles(positions, dim):
    inv = 1.0 / (ROPE_THETA ** (jnp.arange(0, dim, 2, dtype=f32) / dim))
    return positions.astype(f32)[..., None] * inv


def _head_rope_tables(positions):
    ang = _rope_angles(positions, HEAD_DIM)
    cos, sin = jnp.cos(ang), jnp.sin(ang)
    reps = LANES // HEAD_DIM
    cosf = jnp.tile(jnp.concatenate([cos, cos], axis=-1), (1, 1, reps))
    sins = jnp.tile(jnp.concatenate([-sin, sin], axis=-1), (1, 1, reps))
    return cosf, sins


def _retention_decays():
    H, C = RET_HEADS, RET_CHUNK
    log_gamma = jnp.log(1.0 - 2.0 ** (-5.0 - jnp.arange(H, dtype=f32)))
    idx = jnp.arange(C, dtype=f32)
    diff = idx[:, None] - idx[None, :]
    decay_in = jnp.where(diff[None] >= 0, jnp.exp(jnp.maximum(diff, 0.0)[None] * log_gamma[:, None, None]), 0.0)
    q_decay = jnp.exp((idx + 1.0)[None, :] * log_gamma[:, None])
    k_decay = jnp.exp((C - 1.0 - idx)[None, :] * log_gamma[:, None])
    chunk_decay = jnp.exp(C * log_gamma)
    return decay_in, q_decay[..., None], k_decay[..., None], chunk_decay


def kernel(x, positions, l0_attn_norm, l0_w_in, l0_w_out, l0_sinks, l0_ffn_norm, l0_w_a, l0_w_b, l0_conv_w, l0_conv_b, l0_w_down, l1_attn_norm, l1_w_in, l1_w_out, l1_gn_g, l1_ffn_norm, l1_w_a, l1_w_b, l1_conv_w, l1_conv_b, l1_w_down, l2_attn_norm, l2_w_in, l2_w_out, l2_ffn_norm, l2_w_a, l2_w_b, l2_conv_w, l2_conv_b, l2_w_down, l3_attn_norm, l3_w_in, l3_w_out, l3_sinks, l3_ffn_norm, l3_w_a, l3_w_b, l3_conv_w, l3_conv_b, l3_w_down, final_norm):
    mixers = [
        (l0_attn_norm, l0_w_in, l0_w_out, l0_sinks),
        (l1_attn_norm, l1_w_in, l1_w_out, l1_gn_g),
        (l2_attn_norm, l2_w_in, l2_w_out, None),
        (l3_attn_norm, l3_w_in, l3_w_out, l3_sinks),
    ]
    ffns = [
        (l0_ffn_norm, l0_w_a, l0_w_b, l0_conv_w, l0_conv_b, l0_w_down),
        (l1_ffn_norm, l1_w_a, l1_w_b, l1_conv_w, l1_conv_b, l1_w_down),
        (l2_ffn_norm, l2_w_a, l2_w_b, l2_conv_w, l2_conv_b, l2_w_down),
        (l3_ffn_norm, l3_w_a, l3_w_b, l3_conv_w, l3_conv_b, l3_w_down),
    ]
    cosf, sins = _head_rope_tables(positions)
    ang_r = _rope_angles(positions, RET_KEY_DIM).reshape(TOKENS, RET_KEY_DIM // 2)
    cos_r, sin_r = jnp.cos(ang_r), jnp.sin(ang_r)
    decay_in, q_decay, k_decay, chunk_decay = _retention_decays()

    x2d = x.reshape(TOKENS, D_MODEL)
    for i in range(DEPTH):
        norm_g, w_in, w_out, extra = mixers[i]
        m = i % N_MIXERS
        w_in_b = w_in.astype(bf16)
        w_out_b = w_out.astype(bf16)
        if m == 1:
            proj = _ret_proj(x2d, norm_g, w_in_b, cos_r, sin_r)
            y = _ret_core(proj.reshape(BATCH, SEQ, RET_IN_WIDTH), extra, decay_in, q_decay, k_decay, chunk_decay)
            y2d = y.reshape(TOKENS, RET_HEADS * RET_VAL_DIM)
        else:
            q, k, v = _attn_proj(x2d.reshape(BATCH, SEQ, D_MODEL), norm_g, w_in_b, cosf, sins)
            y = _swa(q, k, v, extra) if m == 0 else _moba(q, k, v)
            y2d = y.reshape(TOKENS, D_MODEL)
        x2d = _out_proj(y2d, w_out_b, x2d)
        fg, wa, wb, cw, cb, wd = ffns[i]
        x2d = _ffn(x2d, fg, wa.astype(bf16), wb.astype(bf16), cw, cb, wd.astype(bf16))
    return _final_norm(x2d, final_norm).reshape(BATCH, SEQ, D_MODEL)
```

```python
import functools

import jax
import jax.numpy as jnp
from jax import lax
from jax.experimental import pallas as pl
from jax.experimental.pallas import tpu as pltpu

D_MODEL = 1024
BATCH = 4
SEQ = 4096
DEPTH = 4
N_MIXERS = 3
HEAD_DIM = 64
N_Q_HEADS = D_MODEL // HEAD_DIM
N_KV_HEADS = 4
GQA_GROUP = N_Q_HEADS // N_KV_HEADS
QKV_WIDTH = (N_Q_HEADS + 2 * N_KV_HEADS) * HEAD_DIM
ROPE_THETA = 10000.0
SWA_WINDOW = 128
RET_HEADS = 4
RET_KEY_DIM = D_MODEL // RET_HEADS
RET_VAL_DIM = 2 * RET_KEY_DIM
RET_IN_WIDTH = 2 * D_MODEL + 2 * RET_HEADS * RET_VAL_DIM
RET_CHUNK = 128
MOBA_BLOCK = 256
MOBA_TOPK = 3
D_FF = ((8 * D_MODEL // 3 + 127) // 128) * 128
CONV_WIDTH = 3
EPS = 1e-6
NEG_INF = -1e30

LANES = 128
BF16_ROWS = 16
VMEM_LIMIT_BYTES = 56 * 1024 * 1024

TOKENS = BATCH * SEQ
PROJ_TM = 512
ATT_TQ = 512
RET_TQ = 512
MOBA_ROW_CHUNK = 128
FFN_TM = 512
FFN_CHUNK = 256
FFN_HALO = BF16_ROWS

f32 = jnp.float32
bf16 = jnp.bfloat16


def _params(*semantics):
    return pltpu.CompilerParams(dimension_semantics=semantics, vmem_limit_bytes=VMEM_LIMIT_BYTES)


def _rms_norm_bf16(x, gain):
    ms = jnp.mean(x * x, axis=-1, keepdims=True)
    return (x * lax.rsqrt(ms + EPS) * gain).astype(bf16)


def _silu(x):
    return x / (1.0 + jnp.exp(-x))


def _attn_proj_kernel(x_ref, g_ref, w_ref, cos_ref, sin_ref, q_ref, k_ref, v_ref, *, padded):
    t = pl.program_id(1)
    h = _rms_norm_bf16(x_ref[...], g_ref[...])
    proj = jnp.dot(h, w_ref[...], preferred_element_type=f32)
    cosf = cos_ref[...]
    sins = sin_ref[...]
    lane = lax.broadcasted_iota(jnp.int32, (PROJ_TM, LANES), 1)
    first_half = (lane % HEAD_DIM) < (HEAD_DIM // 2)
    low = lane < HEAD_DIM
    if padded:
        blk = (t * PROJ_TM + lax.broadcasted_iota(jnp.int32, (PROJ_TM, LANES), 0)) // MOBA_BLOCK
        k_tail = jnp.where(lane == HEAD_DIM + blk, 1.0, 0.0)
        v_tail = jnp.where(lane == HEAD_DIM, 1.0, 0.0)

    def split(r, tail):
        if not padded:
            r = r.astype(bf16)
            return r[:, :HEAD_DIM], r[:, HEAD_DIM:]
        return (jnp.where(low, r, tail).astype(bf16),
                jnp.where(low, pltpu.roll(r, HEAD_DIM, 1), tail).astype(bf16))

    n_rope = (N_Q_HEADS + N_KV_HEADS) * HEAD_DIM // LANES
    n_q = N_Q_HEADS * HEAD_DIM // LANES
    for c in range(n_rope):
        xc = proj[:, c * LANES:(c + 1) * LANES]
        partner = jnp.where(first_half, pltpu.roll(xc, LANES - HEAD_DIM // 2, 1), pltpu.roll(xc, HEAD_DIM // 2, 1))
        r = xc * cosf + partner * sins
        if c < n_q:
            q_ref[2 * c], q_ref[2 * c + 1] = split(r * (HEAD_DIM ** -0.5), 0.0)
        else:
            k_ref[2 * (c - n_q)], k_ref[2 * (c - n_q) + 1] = split(r, k_tail if padded else None)
    v0 = n_rope * LANES
    for c in range(N_KV_HEADS * HEAD_DIM // LANES):
        r = proj[:, v0 + c * LANES:v0 + (c + 1) * LANES]
        v_ref[2 * c], v_ref[2 * c + 1] = split(r, v_tail if padded else None)


def _attn_proj(x, gain, w_bf16, cosf, sins, padded):
    nt = SEQ // PROJ_TM
    hw = LANES if padded else HEAD_DIM
    return pl.pallas_call(
        functools.partial(_attn_proj_kernel, padded=padded),
        grid=(BATCH, nt),
        in_specs=[
            pl.BlockSpec((None, PROJ_TM, D_MODEL), lambda b, t: (b, t, 0)),
            pl.BlockSpec((1, D_MODEL), lambda b, t: (0, 0)),
            pl.BlockSpec((D_MODEL, QKV_WIDTH), lambda b, t: (0, 0)),
            pl.BlockSpec((None, PROJ_TM, LANES), lambda b, t: (b, t, 0)),
            pl.BlockSpec((None, PROJ_TM, LANES), lambda b, t: (b, t, 0)),
        ],
        out_specs=[
            pl.BlockSpec((None, N_Q_HEADS, PROJ_TM, hw), lambda b, t: (b, 0, t, 0)),
            pl.BlockSpec((None, N_KV_HEADS, PROJ_TM, hw), lambda b, t: (b, 0, t, 0)),
            pl.BlockSpec((None, N_KV_HEADS, PROJ_TM, hw), lambda b, t: (b, 0, t, 0)),
        ],
        out_shape=[
            jax.ShapeDtypeStruct((BATCH, N_Q_HEADS, SEQ, hw), bf16),
            jax.ShapeDtypeStruct((BATCH, N_KV_HEADS, SEQ, hw), bf16),
            jax.ShapeDtypeStruct((BATCH, N_KV_HEADS, SEQ, hw), bf16),
        ],
        compiler_params=_params("parallel", "parallel"),
        name="attn_proj",
    )(x, gain.reshape(1, D_MODEL), w_bf16, cosf, sins)


def _swa_kernel(sinks_ref, q_ref, k_ref, v_ref, o_ref):
    g = pl.program_id(1)
    t = pl.program_id(2)
    W = SWA_WINDOW
    rows = GQA_GROUP * W
    row = lax.broadcasted_iota(jnp.int32, (rows, 1), 0)
    sink = jnp.zeros((rows, 1), f32)
    for j in range(GQA_GROUP):
        sink = jnp.where(row // W == j, sinks_ref[g * GQA_GROUP + j], sink)
    qi = lax.broadcasted_iota(jnp.int32, (rows, 2 * W), 0) % W
    col = lax.broadcasted_iota(jnp.int32, (rows, 2 * W), 1)
    for c in range(ATT_TQ // W):
        start = t * ATT_TQ + c * W
        kstart = pl.multiple_of(jnp.maximum(start - W, 0), W)
        q4 = q_ref[:, c * W:(c + 1) * W, :].reshape(rows, HEAD_DIM)
        kk = k_ref[pl.ds(kstart, 2 * W), :]
        vv = v_ref[pl.ds(kstart, 2 * W), :]
        s = lax.dot_general(q4, kk, (((1,), (1,)), ((), ())), preferred_element_type=f32)
        rel = (start + qi) - (kstart + col)
        s = jnp.where((rel >= 0) & (rel < W), s, NEG_INF)
        m = jnp.maximum(jnp.max(s, axis=1, keepdims=True), sink)
        p = jnp.exp(s - m)
        denom = jnp.sum(p, axis=1, keepdims=True) + jnp.exp(sink - m)
        o = jnp.dot(p.astype(bf16), vv, preferred_element_type=f32) / denom
        for j in range(GQA_GROUP):
            o_ref[c * W:(c + 1) * W, j * HEAD_DIM:(j + 1) * HEAD_DIM] = o[j * W:(j + 1) * W].astype(bf16)


def _swa(q, k, v, sinks):
    nt = SEQ // ATT_TQ
    gw = GQA_GROUP * HEAD_DIM
    return pl.pallas_call(
        _swa_kernel,
        grid=(BATCH, N_KV_HEADS, nt),
        in_specs=[
            pl.BlockSpec(memory_space=pltpu.SMEM),
            pl.BlockSpec((None, GQA_GROUP, ATT_TQ, HEAD_DIM), lambda b, g, t: (b, g, t, 0)),
            pl.BlockSpec((None, None, SEQ, HEAD_DIM), lambda b, g, t: (b, g, 0, 0)),
            pl.BlockSpec((None, None, SEQ, HEAD_DIM), lambda b, g, t: (b, g, 0, 0)),
        ],
        out_specs=pl.BlockSpec((None, ATT_TQ, gw), lambda b, g, t: (b, t, g)),
        out_shape=jax.ShapeDtypeStruct((BATCH, SEQ, D_MODEL), bf16),
        compiler_params=_params("parallel", "parallel", "parallel"),
        name="swa_core",
    )(sinks, q, k, v)


def _moba_kernel(q_ref, k_ref, v_ref, o_ref, kmean_ref, qa_ref, sa_ref, sb_ref, m_ref, acc_ref):
    i = pl.program_id(2)
    BS = MOBA_BLOCK
    nb = SEQ // BS
    rows = GQA_GROUP * BS
    nt_dims = (((1,), (1,)), ((), ()))

    @pl.when(i == 0)
    def _():
        lane = lax.broadcasted_iota(jnp.int32, (1, LANES), 1)
        for n in range(nb):
            kb = k_ref[n * BS:(n + 1) * BS, :].astype(f32)
            kmean_ref[n:n + 1, :] = jnp.where(lane < HEAD_DIM, jnp.mean(kb, axis=0, keepdims=True), 0.0)

    q = q_ref[...].reshape(rows, LANES)
    gate_t = lax.dot_general(kmean_ref[...].astype(bf16), q, nt_dims, preferred_element_type=f32)
    blk = lax.broadcasted_iota(jnp.int32, (nb, rows), 0)
    past = blk < i
    gm = jnp.where(past, gate_t, NEG_INF)
    sel = jnp.zeros((nb, rows), jnp.int32)
    for _ in range(MOBA_TOPK):
        mx = jnp.max(gm, axis=0, keepdims=True)
        cand = (gm == mx) & past & (sel == 0)
        idx = jnp.min(jnp.where(cand, blk, nb), axis=0, keepdims=True)
        pick = blk == idx
        sel = jnp.where(pick, 1, sel)
        gm = jnp.where(pick, NEG_INF, gm)
    bias_t = jnp.where((sel == 1) | jnp.logical_not(past), 0.0, NEG_INF)
    bias_pad_t = jnp.concatenate(
        [jnp.zeros((HEAD_DIM, rows), f32), bias_t, jnp.zeros((LANES - HEAD_DIM - nb, rows), f32)], axis=0)
    qa_ref[...] = (q.astype(f32) + bias_pad_t.T).astype(bf16)

    RC = MOBA_ROW_CHUNK
    qi = lax.broadcasted_iota(jnp.int32, (RC, LANES), 0)
    kj = lax.broadcasted_iota(jnp.int32, (RC, LANES), 1)

    def block_start(n):
        return pl.multiple_of(jnp.minimum(n, nb - 1) * BS, BS)

    def scores(n, s_ref):
        k_n = k_ref[pl.ds(block_start(n), BS), :]
        s_ref[...] = lax.dot_general(qa_ref[...], k_n, nt_dims, preferred_element_type=f32)

    def softmax_pv(n, s_ref, own):
        v_n = v_ref[pl.ds(block_start(n), BS), :]
        for c in range(rows // RC):
            rs = slice(c * RC, (c + 1) * RC)
            s = s_ref[rs, :]
            s0, s1 = s[:, :LANES], s[:, LANES:]
            if own:
                q0 = (c * RC) % BS
                s0 = jnp.where(kj <= qi + q0, s0, NEG_INF)
                s1 = jnp.where(kj + LANES <= qi + q0, s1, NEG_INF)
            row_max = jnp.max(jnp.maximum(s0, s1), axis=1, keepdims=True)
            if own:
                m_new = jnp.broadcast_to(row_max, (RC, LANES))
            else:
                m_old = m_ref[rs, :]
                m_new = jnp.maximum(m_old, row_max)
            p = jnp.concatenate([jnp.exp(s0 - m_new), jnp.exp(s1 - m_new)], axis=1).astype(bf16)
            pv = jnp.dot(p, v_n, preferred_element_type=f32)
            if own:
                acc_ref[rs, :] = pv
            else:
                acc_ref[rs, :] = jnp.exp(m_old - m_new) * acc_ref[rs, :] + pv
            m_ref[rs, :] = m_new

    scores(i, sa_ref)
    scores(0, sb_ref)
    softmax_pv(i, sa_ref, True)

    def body(n, carry):
        @pl.when(n % 2 == 0)
        def _():
            scores(n + 1, sa_ref)
            softmax_pv(n, sb_ref, False)

        @pl.when(n % 2 == 1)
        def _():
            scores(n + 1, sb_ref)
            softmax_pv(n, sa_ref, False)

        return carry

    lax.fori_loop(0, i, body, 0)
    for j in range(GQA_GROUP):
        a = acc_ref[j * BS:(j + 1) * BS, :]
        o_ref[:, j * HEAD_DIM:(j + 1) * HEAD_DIM] = (a[:, :HEAD_DIM] / a[:, HEAD_DIM:HEAD_DIM + 1]).astype(bf16)


def _moba(q, k, v):
    nb = SEQ // MOBA_BLOCK
    gw = GQA_GROUP * HEAD_DIM
    rows = GQA_GROUP * MOBA_BLOCK
    return pl.pallas_call(
        _moba_kernel,
        grid=(BATCH, N_KV_HEADS, nb),
        in_specs=[
            pl.BlockSpec((None, GQA_GROUP, MOBA_BLOCK, LANES), lambda b, g, i: (b, g, i, 0)),
            pl.BlockSpec((None, None, SEQ, LANES), lambda b, g, i: (b, g, 0, 0)),
            pl.BlockSpec((None, None, SEQ, LANES), lambda b, g, i: (b, g, 0, 0)),
        ],
        out_specs=pl.BlockSpec((None, MOBA_BLOCK, gw), lambda b, g, i: (b, i, g)),
        out_shape=jax.ShapeDtypeStruct((BATCH, SEQ, D_MODEL), bf16),
        scratch_shapes=[
            pltpu.VMEM((nb, LANES), f32),
            pltpu.VMEM((rows, LANES), bf16),
            pltpu.VMEM((rows, MOBA_BLOCK), f32),
            pltpu.VMEM((rows, MOBA_BLOCK), f32),
            pltpu.VMEM((rows, LANES), f32),
            pltpu.VMEM((rows, LANES), f32),
        ],
        compiler_params=_params("parallel", "parallel", "arbitrary"),
        name="moba_core",
    )(q, k, v)


RET_TN = 1024


def _ret_proj_kernel(x_ref, g_ref, w_ref, cos_ref, sin_ref, o_ref, h_ref):
    j = pl.program_id(1)

    @pl.when(j == 0)
    def _():
        h_ref[...] = _rms_norm_bf16(x_ref[...], g_ref[...])

    proj = jnp.dot(h_ref[...], w_ref[...], preferred_element_type=f32)

    @pl.when(j < 2)
    def _():
        cos = cos_ref[...]
        sin = sin_ref[...]
        scale = jnp.where(j == 1, RET_KEY_DIM ** -0.5, 1.0)
        half = RET_KEY_DIM // 2
        for hh in range(RET_TN // RET_KEY_DIM):
            x1 = proj[:, hh * RET_KEY_DIM:hh * RET_KEY_DIM + half]
            x2 = proj[:, hh * RET_KEY_DIM + half:(hh + 1) * RET_KEY_DIM]
            o_ref[:, hh * RET_KEY_DIM:hh * RET_KEY_DIM + half] = ((x1 * cos - x2 * sin) * scale).astype(bf16)
            o_ref[:, hh * RET_KEY_DIM + half:(hh + 1) * RET_KEY_DIM] = ((x2 * cos + x1 * sin) * scale).astype(bf16)

    @pl.when(j >= 2)
    def _():
        o_ref[...] = proj.astype(bf16)


def _ret_proj(x2d, gain, w_bf16, cos, sin):
    nt = TOKENS // PROJ_TM
    nj = RET_IN_WIDTH // RET_TN
    return pl.pallas_call(
        _ret_proj_kernel,
        grid=(nt, nj),
        in_specs=[
            pl.BlockSpec((PROJ_TM, D_MODEL), lambda t, j: (t, 0)),
            pl.BlockSpec((1, D_MODEL), lambda t, j: (0, 0)),
            pl.BlockSpec((D_MODEL, RET_TN), lambda t, j: (0, j)),
            pl.BlockSpec((PROJ_TM, LANES), lambda t, j: (t, 0)),
            pl.BlockSpec((PROJ_TM, LANES), lambda t, j: (t, 0)),
        ],
        out_specs=pl.BlockSpec((PROJ_TM, RET_TN), lambda t, j: (t, j)),
        out_shape=jax.ShapeDtypeStruct((TOKENS, RET_IN_WIDTH), bf16),
        scratch_shapes=[pltpu.VMEM((PROJ_TM, D_MODEL), bf16)],
        compiler_params=_params("parallel", "arbitrary"),
        name="ret_proj",
    )(x2d, gain.reshape(1, D_MODEL), w_bf16, cos, sin)


def _ret_core_kernel(cd_ref, q_ref, k_ref, v_ref, g_ref, din_ref, qdec_ref, kdec_ref, gn_ref, y_ref, r_ref):
    h = pl.program_id(1)
    t = pl.program_id(2)
    C = RET_CHUNK

    @pl.when(t == 0)
    def _():
        r_ref[...] = jnp.zeros_like(r_ref)

    decay_in = din_ref[...]
    qdec = qdec_ref[...]
    kdec = kdec_ref[...]
    cd = cd_ref[h]
    gn = gn_ref[...]
    for c in range(RET_TQ // C):
        sl = slice(c * C, (c + 1) * C)
        qn = q_ref[sl, :]
        kn = k_ref[sl, :]
        vn = v_ref[sl, :]
        sc = lax.dot_general(qn, kn, (((1,), (1,)), ((), ())), preferred_element_type=f32) * decay_in
        inner = jnp.dot(sc.astype(bf16), vn, preferred_element_type=f32)
        r_old = r_ref[...]
        qd = (qn.astype(f32) * qdec).astype(bf16)
        cross = jnp.dot(qd, r_old.astype(bf16), preferred_element_type=f32)
        kdt = (kn.astype(f32) * kdec).T.astype(bf16)
        r_ref[...] = cd * r_old + jnp.dot(kdt, vn, preferred_element_type=f32)
        o = inner + cross
        mu = jnp.mean(o, axis=-1, keepdims=True)
        d = o - mu
        var = jnp.mean(d * d, axis=-1, keepdims=True)
        on = d * lax.rsqrt(var + EPS) * gn
        y_ref[sl, :] = (_silu(g_ref[sl, :].astype(f32)) * on).astype(bf16)


def _ret_core(proj3d, gn_g, decay_in, q_decay, k_decay, chunk_decay):
    nt = SEQ // RET_TQ
    H, dk, dv, C = RET_HEADS, RET_KEY_DIM, RET_VAL_DIM, RET_CHUNK
    v_blk0 = 2 * D_MODEL // dv
    g_blk0 = (2 * D_MODEL + H * dv) // dv
    return pl.pallas_call(
        _ret_core_kernel,
        grid=(BATCH, H, nt),
        in_specs=[
            pl.BlockSpec(memory_space=pltpu.SMEM),
            pl.BlockSpec((None, RET_TQ, dk), lambda b, h, t: (b, t, h)),
            pl.BlockSpec((None, RET_TQ, dk), lambda b, h, t: (b, t, H + h)),
            pl.BlockSpec((None, RET_TQ, dv), lambda b, h, t: (b, t, v_blk0 + h)),
            pl.BlockSpec((None, RET_TQ, dv), lambda b, h, t: (b, t, g_blk0 + h)),
            pl.BlockSpec((None, C, C), lambda b, h, t: (h, 0, 0)),
            pl.BlockSpec((None, C, 1), lambda b, h, t: (h, 0, 0)),
            pl.BlockSpec((None, C, 1), lambda b, h, t: (h, 0, 0)),
            pl.BlockSpec((1, dv), lambda b, h, t: (0, h)),
        ],
        out_specs=pl.BlockSpec((None, RET_TQ, dv), lambda b, h, t: (b, t, h)),
        out_shape=jax.ShapeDtypeStruct((BATCH, SEQ, H * dv), bf16),
        scratch_shapes=[pltpu.VMEM((dk, dv), f32)],
        compiler_params=_params("parallel", "parallel", "arbitrary"),
        name="ret_core",
    )(chunk_decay, proj3d, proj3d, proj3d, proj3d, decay_in, q_decay, k_decay, gn_g.reshape(1, H * dv))


def _out_proj_kernel(a_ref, w_ref, x_ref, o_ref):
    o_ref[...] = x_ref[...] + jnp.dot(a_ref[...], w_ref[...], preferred_element_type=f32)


def _out_proj(a2d, w_bf16, x2d):
    kdim = a2d.shape[1]
    nt = TOKENS // PROJ_TM
    return pl.pallas_call(
        _out_proj_kernel,
        grid=(nt,),
        in_specs=[
            pl.BlockSpec((PROJ_TM, kdim), lambda t: (t, 0)),
            pl.BlockSpec((kdim, D_MODEL), lambda t: (0, 0)),
            pl.BlockSpec((PROJ_TM, D_MODEL), lambda t: (t, 0)),
        ],
        out_specs=pl.BlockSpec((PROJ_TM, D_MODEL), lambda t: (t, 0)),
        out_shape=jax.ShapeDtypeStruct((TOKENS, D_MODEL), f32),
        compiler_params=_params("parallel"),
        name="out_proj",
    )(a2d, w_bf16, x2d)


def _ffn_kernel(x_ref, xh_ref, g_ref, wa_ref, wb_ref, cw_ref, cb_ref, wd_ref, o_ref):
    t = pl.program_id(0)
    xt = x_ref[...]
    gain = g_ref[...]
    h_tile = _rms_norm_bf16(xt, gain)
    h_halo = _rms_norm_bf16(xh_ref[...], gain)
    h_full = jnp.concatenate([h_halo, h_tile], axis=0)
    halo_valid = jnp.where((t * FFN_TM) % SEQ == 0, 0.0, 1.0)
    row = lax.broadcasted_iota(jnp.int32, (FFN_HALO + FFN_TM, 1), 0)
    keep = jnp.where(row < FFN_HALO, halo_valid, 1.0)
    acc = xt
    for c in range(D_FF // FFN_CHUNK):
        cs = slice(c * FFN_CHUNK, (c + 1) * FFN_CHUNK)
        a = jnp.dot(h_full, wa_ref[:, cs], preferred_element_type=f32) * keep
        b = jnp.dot(h_tile, wb_ref[:, cs], preferred_element_type=f32)
        a1 = pltpu.roll(a, 1, 0)[FFN_HALO:]
        a2 = pltpu.roll(a, 2, 0)[FFN_HALO:]
        cw = cw_ref[:, cs]
        conv = cw[0:1] * a2 + cw[1:2] * a1 + cw[2:3] * a[FFN_HALO:] + cb_ref[:, cs]
        hid = (_silu(conv) * b).astype(bf16)
        acc = acc + jnp.dot(hid, wd_ref[cs, :], preferred_element_type=f32)
    o_ref[...] = acc


def _ffn(x2d, gain, wa, wb, conv_w, conv_b, wd):
    nt = TOKENS // FFN_TM
    halo_blocks = FFN_TM // FFN_HALO
    resident = dict(pipeline_mode=pl.Buffered(1))
    return pl.pallas_call(
        _ffn_kernel,
        grid=(nt,),
        in_specs=[
            pl.BlockSpec((FFN_TM, D_MODEL), lambda t: (t, 0)),
            pl.BlockSpec((FFN_HALO, D_MODEL), lambda t: (jnp.maximum(t * halo_blocks - 1, 0), 0)),
            pl.BlockSpec((1, D_MODEL), lambda t: (0, 0)),
            pl.BlockSpec((D_MODEL, D_FF), lambda t: (0, 0), **resident),
            pl.BlockSpec((D_MODEL, D_FF), lambda t: (0, 0), **resident),
            pl.BlockSpec((CONV_WIDTH, D_FF), lambda t: (0, 0)),
            pl.BlockSpec((1, D_FF), lambda t: (0, 0)),
            pl.BlockSpec((D_FF, D_MODEL), lambda t: (0, 0), **resident),
        ],
        out_specs=pl.BlockSpec((FFN_TM, D_MODEL), lambda t: (t, 0)),
        out_shape=jax.ShapeDtypeStruct((TOKENS, D_MODEL), f32),
        compiler_params=_params("parallel"),
        name="conv_ffn",
    )(x2d, x2d, gain.reshape(1, D_MODEL), wa, wb, conv_w, conv_b.reshape(1, D_FF), wd)


def _final_norm_kernel(x_ref, g_ref, o_ref):
    x = x_ref[...]
    ms = jnp.mean(x * x, axis=-1, keepdims=True)
    o_ref[...] = x * lax.rsqrt(ms + EPS) * g_ref[...]


def _final_norm(x2d, gain):
    nt = TOKENS // PROJ_TM
    return pl.pallas_call(
        _final_norm_kernel,
        grid=(nt,),
        in_specs=[
            pl.BlockSpec((PROJ_TM, D_MODEL), lambda t: (t, 0)),
            pl.BlockSpec((1, D_MODEL), lambda t: (0, 0)),
        ],
        out_specs=pl.BlockSpec((PROJ_TM, D_MODEL), lambda t: (t, 0)),
        out_shape=jax.ShapeDtypeStruct((TOKENS, D_MODEL), f32),
        compiler_params=_params("parallel"),
        name="final_norm",
    )(x2d, gain.reshape(1, D_MODEL))


def _rope_angles(positions, dim):
    inv = 1.0 / (ROPE_THETA ** (jnp.arange(0, dim, 2, dtype=f32) / dim))
    return positions.astype(f32)[..., None] * inv


def _head_rope_tables(positions):
    ang = _rope_angles(positions, HEAD_DIM)
    cos, sin = jnp.cos(ang), jnp.sin(ang)
    reps = LANES // HEAD_DIM
    cosf = jnp.tile(jnp.concatenate([cos, cos], axis=-1), (1, 1, reps))
    sins = jnp.tile(jnp.concatenate([-sin, sin], axis=-1), (1, 1, reps))
    return cosf, sins


def _retention_decays():
    H, C = RET_HEADS, RET_CHUNK
    log_gamma = jnp.log(1.0 - 2.0 ** (-5.0 - jnp.arange(H, dtype=f32)))
    idx = jnp.arange(C, dtype=f32)
    diff = idx[:, None] - idx[None, :]
    decay_in = jnp.where(diff[None] >= 0, jnp.exp(jnp.maximum(diff, 0.0)[None] * log_gamma[:, None, None]), 0.0)
    q_decay = jnp.exp((idx + 1.0)[None, :] * log_gamma[:, None])
    k_decay = jnp.exp((C - 1.0 - idx)[None, :] * log_gamma[:, None])
    chunk_decay = jnp.exp(C * log_gamma)
    return decay_in, q_decay[..., None], k_decay[..., None], chunk_decay


def kernel(x, positions, l0_attn_norm, l0_w_in, l0_w_out, l0_sinks, l0_ffn_norm, l0_w_a, l0_w_b, l0_conv_w, l0_conv_b, l0_w_down, l1_attn_norm, l1_w_in, l1_w_out, l1_gn_g, l1_ffn_norm, l1_w_a, l1_w_b, l1_conv_w, l1_conv_b, l1_w_down, l2_attn_norm, l2_w_in, l2_w_out, l2_ffn_norm, l2_w_a, l2_w_b, l2_conv_w, l2_conv_b, l2_w_down, l3_attn_norm, l3_w_in, l3_w_out, l3_sinks, l3_ffn_norm, l3_w_a, l3_w_b, l3_conv_w, l3_conv_b, l3_w_down, final_norm):
    mixers = [
        (l0_attn_norm, l0_w_in, l0_w_out, l0_sinks),
        (l1_attn_norm, l1_w_in, l1_w_out, l1_gn_g),
        (l2_attn_norm, l2_w_in, l2_w_out, None),
        (l3_attn_norm, l3_w_in, l3_w_out, l3_sinks),
    ]
    ffns = [
        (l0_ffn_norm, l0_w_a, l0_w_b, l0_conv_w, l0_conv_b, l0_w_down),
        (l1_ffn_norm, l1_w_a, l1_w_b, l1_conv_w, l1_conv_b, l1_w_down),
        (l2_ffn_norm, l2_w_a, l2_w_b, l2_conv_w, l2_conv_b, l2_w_down),
        (l3_ffn_norm, l3_w_a, l3_w_b, l3_conv_w, l3_conv_b, l3_w_down),
    ]
    cosf, sins = _head_rope_tables(positions)
    ang_r = _rope_angles(positions, RET_KEY_DIM).reshape(TOKENS, RET_KEY_DIM // 2)
    cos_r, sin_r = jnp.cos(ang_r), jnp.sin(ang_r)
    decay_in, q_decay, k_decay, chunk_decay = _retention_decays()

    x2d = x.reshape(TOKENS, D_MODEL)
    for i in range(DEPTH):
        norm_g, w_in, w_out, extra = mixers[i]
        m = i % N_MIXERS
        w_in_b = w_in.astype(bf16)
        w_out_b = w_out.astype(bf16)
        if m == 1:
            proj = _ret_proj(x2d, norm_g, w_in_b, cos_r, sin_r)
            y = _ret_core(proj.reshape(BATCH, SEQ, RET_IN_WIDTH), extra, decay_in, q_decay, k_decay, chunk_decay)
            y2d = y.reshape(TOKENS, RET_HEADS * RET_VAL_DIM)
        else:
            q, k, v = _attn_proj(x2d.reshape(BATCH, SEQ, D_MODEL), norm_g, w_in_b, cosf, sins, padded=(m == 2))
            y = _swa(q, k, v, extra) if m == 0 else _moba(q, k, v)
            y2d = y.reshape(TOKENS, D_MODEL)
        x2d = _out_proj(y2d, w_out_b, x2d)
        fg, wa, wb, cw, cb, wd = ffns[i]
        x2d = _ffn(x2d, fg, wa.astype(bf16), wb.astype(bf16), cw, cb, wd.astype(bf16))
    return _final_norm(x2d, final_norm).reshape(BATCH, SEQ, D_MODEL)
```

```python
import functools

import jax
import jax.numpy as jnp
from jax import lax
from jax.experimental import pallas as pl
from jax.experimental.pallas import tpu as pltpu

D_MODEL = 1024
BATCH = 4
SEQ = 4096
DEPTH = 4
N_MIXERS = 3
HEAD_DIM = 64
N_Q_HEADS = D_MODEL // HEAD_DIM
N_KV_HEADS = 4
GQA_GROUP = N_Q_HEADS // N_KV_HEADS
QKV_WIDTH = (N_Q_HEADS + 2 * N_KV_HEADS) * HEAD_DIM
ROPE_THETA = 10000.0
SWA_WINDOW = 128
RET_HEADS = 4
RET_KEY_DIM = D_MODEL // RET_HEADS
RET_VAL_DIM = 2 * RET_KEY_DIM
RET_IN_WIDTH = 2 * D_MODEL + 2 * RET_HEADS * RET_VAL_DIM
RET_CHUNK = 128
MOBA_BLOCK = 256
MOBA_TOPK = 3
D_FF = ((8 * D_MODEL // 3 + 127) // 128) * 128
CONV_WIDTH = 3
EPS = 1e-6
NEG_INF = -1e30

LANES = 128
SUBLANES = 8
MXU_WIDTH = 256
VMEM_LIMIT_BYTES = 56 * 1024 * 1024

TOKENS = BATCH * SEQ
PROJ_TM = 512
ATT_TQ = 512
RET_TQ = 512
MOBA_ROW_CHUNK = 128
FFN_TM = 512
FFN_CHUNK = 256
FFN_HALO = SUBLANES

f32 = jnp.float32
bf16 = jnp.bfloat16
NT_DIMS = (((1,), (1,)), ((), ()))


def _params(*semantics):
    return pltpu.CompilerParams(dimension_semantics=semantics, vmem_limit_bytes=VMEM_LIMIT_BYTES)


def _rms_norm(x, gain):
    ms = jnp.mean(x * x, axis=-1, keepdims=True)
    return x * lax.rsqrt(ms + EPS) * gain


def _silu(x):
    return x * (0.5 + 0.5 * jnp.tanh(0.5 * x))


def _normalize_pair(pv_even, pv_odd, extra_even, extra_odd):
    low = lax.broadcasted_iota(jnp.int32, pv_even.shape, 1) < HEAD_DIM
    num = jnp.where(low, pv_even, pltpu.roll(pv_odd, HEAD_DIM, 1))
    den = jnp.where(low, pltpu.roll(pv_even, HEAD_DIM, 1) + extra_even, pv_odd + extra_odd)
    return num / den


def _attn_proj_kernel(x_ref, g_ref, w_ref, cos_ref, sin_ref, q_ref, k_ref, v_ref):
    t = pl.program_id(1)
    h = _rms_norm(x_ref[...], g_ref[...]).astype(bf16)
    cosf = cos_ref[...]
    sins = sin_ref[...]
    lane = lax.broadcasted_iota(jnp.int32, (PROJ_TM, LANES), 1)
    first_half = (lane % HEAD_DIM) < (HEAD_DIM // 2)
    low = lane < HEAD_DIM
    blk = (t * PROJ_TM + lax.broadcasted_iota(jnp.int32, (PROJ_TM, LANES), 0)) // MOBA_BLOCK
    k_tail = jnp.where(lane == HEAD_DIM + blk, 1.0, 0.0)
    v_tail = 1.0

    def split(r, tail):
        return (jnp.where(low, r, tail).astype(bf16),
                jnp.where(low, pltpu.roll(r, HEAD_DIM, 1), tail).astype(bf16))

    def rope(xc):
        partner = jnp.where(first_half, pltpu.roll(xc, LANES - HEAD_DIM // 2, 1), pltpu.roll(xc, HEAD_DIM // 2, 1))
        return xc * cosf + partner * sins

    n_q = N_Q_HEADS * HEAD_DIM // LANES
    n_k = N_KV_HEADS * HEAD_DIM // LANES
    groups_per_dot = MXU_WIDTH // LANES
    for d in range(QKV_WIDTH // MXU_WIDTH):
        proj = jnp.dot(h, w_ref[:, d * MXU_WIDTH:(d + 1) * MXU_WIDTH], preferred_element_type=f32)
        for e in range(groups_per_dot):
            c = d * groups_per_dot + e
            xc = proj[:, e * LANES:(e + 1) * LANES]
            if c < n_q:
                q_ref[2 * c], q_ref[2 * c + 1] = split(rope(xc) * (HEAD_DIM ** -0.5), 0.0)
            elif c < n_q + n_k:
                k_ref[2 * (c - n_q)], k_ref[2 * (c - n_q) + 1] = split(rope(xc), k_tail)
            else:
                v_ref[2 * (c - n_q - n_k)], v_ref[2 * (c - n_q - n_k) + 1] = split(xc, v_tail)


def _attn_proj(x, gain, w_bf16, cosf, sins):
    nt = SEQ // PROJ_TM
    return pl.pallas_call(
        _attn_proj_kernel,
        grid=(BATCH, nt),
        in_specs=[
            pl.BlockSpec((None, PROJ_TM, D_MODEL), lambda b, t: (b, t, 0)),
            pl.BlockSpec((1, D_MODEL), lambda b, t: (0, 0)),
            pl.BlockSpec((D_MODEL, QKV_WIDTH), lambda b, t: (0, 0)),
            pl.BlockSpec((None, PROJ_TM, LANES), lambda b, t: (b, t, 0)),
            pl.BlockSpec((None, PROJ_TM, LANES), lambda b, t: (b, t, 0)),
        ],
        out_specs=[
            pl.BlockSpec((None, N_Q_HEADS, PROJ_TM, LANES), lambda b, t: (b, 0, t, 0)),
            pl.BlockSpec((None, N_KV_HEADS, PROJ_TM, LANES), lambda b, t: (b, 0, t, 0)),
            pl.BlockSpec((None, N_KV_HEADS, PROJ_TM, LANES), lambda b, t: (b, 0, t, 0)),
        ],
        out_shape=[
            jax.ShapeDtypeStruct((BATCH, N_Q_HEADS, SEQ, LANES), bf16),
            jax.ShapeDtypeStruct((BATCH, N_KV_HEADS, SEQ, LANES), bf16),
            jax.ShapeDtypeStruct((BATCH, N_KV_HEADS, SEQ, LANES), bf16),
        ],
        compiler_params=_params("parallel", "parallel"),
        name="attn_proj",
    )(x, gain.reshape(1, D_MODEL), w_bf16, cosf, sins)


def _swa_kernel(sinks_ref, q_ref, k_ref, v_ref, o_ref, s_ref):
    g = pl.program_id(1)
    t = pl.program_id(2)
    W = SWA_WINDOW
    nsub = ATT_TQ // W
    qi = lax.broadcasted_iota(jnp.int32, (W, 2 * W), 0)
    col = lax.broadcasted_iota(jnp.int32, (W, 2 * W), 1)
    band = jnp.where((col > qi) & (col <= qi + W), 0.0, NEG_INF)
    first = jnp.where(col <= qi, 0.0, NEG_INF)

    def key_start(c):
        return pl.multiple_of(jnp.maximum(t * ATT_TQ + (c - 1) * W, 0), W)

    def scores(c):
        q4 = q_ref[:, c * W:(c + 1) * W, :].reshape(GQA_GROUP * W, LANES)
        kk = k_ref[pl.ds(key_start(c), 2 * W), :]
        s_ref[c] = lax.dot_general(q4, kk, NT_DIMS, preferred_element_type=f32)

    def head(c, j, vv, bias):
        sink = sinks_ref[g * GQA_GROUP + j]
        s = s_ref[c, j * W:(j + 1) * W, :] + bias
        s0, s1 = s[:, :LANES], s[:, LANES:]
        m = jnp.maximum(jnp.max(jnp.maximum(s0, s1), axis=1, keepdims=True), sink)
        p = jnp.concatenate([jnp.exp(s0 - m), jnp.exp(s1 - m)], axis=1).astype(bf16)
        return jnp.dot(p, vv, preferred_element_type=f32), jnp.exp(sink - m)

    scores(0)
    for c in range(nsub):
        if c + 1 < nsub:
            scores(c + 1)
        vv = v_ref[pl.ds(key_start(c), 2 * W), :]
        bias = jnp.where(t == 0, first, band) if c == 0 else band
        for jj in range(GQA_GROUP // 2):
            pv0, e0 = head(c, 2 * jj, vv, bias)
            pv1, e1 = head(c, 2 * jj + 1, vv, bias)
            o_ref[c * W:(c + 1) * W, jj * LANES:(jj + 1) * LANES] = _normalize_pair(pv0, pv1, e0, e1).astype(bf16)


def _swa(q, k, v, sinks):
    nt = SEQ // ATT_TQ
    gw = GQA_GROUP * HEAD_DIM
    return pl.pallas_call(
        _swa_kernel,
        grid=(BATCH, N_KV_HEADS, nt),
        in_specs=[
            pl.BlockSpec(memory_space=pltpu.SMEM),
            pl.BlockSpec((None, GQA_GROUP, ATT_TQ, LANES), lambda b, g, t: (b, g, t, 0)),
            pl.BlockSpec((None, None, SEQ, LANES), lambda b, g, t: (b, g, 0, 0)),
            pl.BlockSpec((None, None, SEQ, LANES), lambda b, g, t: (b, g, 0, 0)),
        ],
        out_specs=pl.BlockSpec((None, ATT_TQ, gw), lambda b, g, t: (b, t, g)),
        out_shape=jax.ShapeDtypeStruct((BATCH, SEQ, D_MODEL), bf16),
        scratch_shapes=[pltpu.VMEM((ATT_TQ // SWA_WINDOW, GQA_GROUP * SWA_WINDOW, 2 * SWA_WINDOW), f32)],
        compiler_params=_params("parallel", "parallel", "parallel"),
        name="swa_core",
    )(sinks, q, k, v)


def _moba_kernel(q_ref, k_ref, v_ref, o_ref, kmean_ref, qa_ref, sa_ref, sb_ref, m_ref, acc_ref):
    i = pl.program_id(2)
    BS = MOBA_BLOCK
    nb = SEQ // BS
    rows = GQA_GROUP * BS

    @pl.when(i == 0)
    def _():
        lane = lax.broadcasted_iota(jnp.int32, (1, LANES), 1)
        for n in range(nb):
            kb = k_ref[n * BS:(n + 1) * BS, :].astype(f32)
            kmean_ref[n:n + 1, :] = jnp.where(lane < HEAD_DIM, jnp.mean(kb, axis=0, keepdims=True), 0.0)

    q = q_ref[...].reshape(rows, LANES)
    gate_t = lax.dot_general(kmean_ref[...].astype(bf16), q, NT_DIMS, preferred_element_type=f32)
    blk = lax.broadcasted_iota(jnp.int32, (nb, rows), 0)
    past = blk < i
    gm = jnp.where(past, gate_t, NEG_INF)
    sel = jnp.zeros((nb, rows), jnp.int32)
    for _ in range(MOBA_TOPK):
        mx = jnp.max(gm, axis=0, keepdims=True)
        cand = (gm == mx) & past & (sel == 0)
        idx = jnp.min(jnp.where(cand, blk, nb), axis=0, keepdims=True)
        pick = blk == idx
        sel = jnp.where(pick, 1, sel)
        gm = jnp.where(pick, NEG_INF, gm)
    bias_t = jnp.where((sel == 1) | jnp.logical_not(past), 0.0, NEG_INF)
    bias_pad_t = jnp.concatenate(
        [jnp.zeros((HEAD_DIM, rows), f32), bias_t, jnp.zeros((LANES - HEAD_DIM - nb, rows), f32)], axis=0)
    qa_ref[...] = (q.astype(f32) + bias_pad_t.T).astype(bf16)

    RC = MOBA_ROW_CHUNK
    qi = lax.broadcasted_iota(jnp.int32, (RC, LANES), 0)
    kj = lax.broadcasted_iota(jnp.int32, (RC, LANES), 1)

    def block_start(n):
        return pl.multiple_of(jnp.minimum(n, nb - 1) * BS, BS)

    def scores(n, s_ref):
        k_n = k_ref[pl.ds(block_start(n), BS), :]
        s_ref[...] = lax.dot_general(qa_ref[...], k_n, NT_DIMS, preferred_element_type=f32)

    def softmax_pv(n, s_ref, own):
        v_n = v_ref[pl.ds(block_start(n), BS), :]
        for c in range(rows // RC):
            rs = slice(c * RC, (c + 1) * RC)
            s = s_ref[rs, :]
            s0, s1 = s[:, :LANES], s[:, LANES:]
            if own:
                q0 = (c * RC) % BS
                s0 = jnp.where(kj <= qi + q0, s0, NEG_INF)
                s1 = jnp.where(kj + LANES <= qi + q0, s1, NEG_INF)
            row_max = jnp.max(jnp.maximum(s0, s1), axis=1, keepdims=True)
            if own:
                m_new = jnp.broadcast_to(row_max, (RC, LANES))
            else:
                m_old = m_ref[rs, :]
                m_new = jnp.maximum(m_old, row_max)
            p = jnp.concatenate([jnp.exp(s0 - m_new), jnp.exp(s1 - m_new)], axis=1).astype(bf16)
            pv = jnp.dot(p, v_n, preferred_element_type=f32)
            if own:
                acc_ref[rs, :] = pv
            else:
                acc_ref[rs, :] = jnp.exp(m_old - m_new) * acc_ref[rs, :] + pv
            m_ref[rs, :] = m_new

    scores(i, sa_ref)
    scores(0, sb_ref)
    softmax_pv(i, sa_ref, True)

    def body(n, carry):
        @pl.when(n % 2 == 0)
        def _():
            scores(n + 1, sa_ref)
            softmax_pv(n, sb_ref, False)

        @pl.when(n % 2 == 1)
        def _():
            scores(n + 1, sb_ref)
            softmax_pv(n, sa_ref, False)

        return carry

    lax.fori_loop(0, i, body, 0)
    for jj in range(GQA_GROUP // 2):
        a0 = acc_ref[2 * jj * BS:(2 * jj + 1) * BS, :]
        a1 = acc_ref[(2 * jj + 1) * BS:(2 * jj + 2) * BS, :]
        o_ref[:, jj * LANES:(jj + 1) * LANES] = _normalize_pair(a0, a1, 0.0, 0.0).astype(bf16)


def _moba(q, k, v):
    nb = SEQ // MOBA_BLOCK
    gw = GQA_GROUP * HEAD_DIM
    rows = GQA_GROUP * MOBA_BLOCK
    return pl.pallas_call(
        _moba_kernel,
        grid=(BATCH, N_KV_HEADS, nb),
        in_specs=[
            pl.BlockSpec((None, GQA_GROUP, MOBA_BLOCK, LANES), lambda b, g, i: (b, g, i, 0)),
            pl.BlockSpec((None, None, SEQ, LANES), lambda b, g, i: (b, g, 0, 0)),
            pl.BlockSpec((None, None, SEQ, LANES), lambda b, g, i: (b, g, 0, 0)),
        ],
        out_specs=pl.BlockSpec((None, MOBA_BLOCK, gw), lambda b, g, i: (b, i, g)),
        out_shape=jax.ShapeDtypeStruct((BATCH, SEQ, D_MODEL), bf16),
        scratch_shapes=[
            pltpu.VMEM((nb, LANES), f32),
            pltpu.VMEM((rows, LANES), bf16),
            pltpu.VMEM((rows, MOBA_BLOCK), f32),
            pltpu.VMEM((rows, MOBA_BLOCK), f32),
            pltpu.VMEM((rows, LANES), f32),
            pltpu.VMEM((rows, LANES), f32),
        ],
        compiler_params=_params("parallel", "parallel", "arbitrary"),
        name="moba_core",
    )(q, k, v)


RET_TN = 1024


def _ret_proj_kernel(x_ref, g_ref, w_ref, cos_ref, sin_ref, o_ref, h_ref):
    j = pl.program_id(1)

    @pl.when(j == 0)
    def _():
        h_ref[...] = _rms_norm(x_ref[...], g_ref[...]).astype(bf16)

    def slab(c):
        return jnp.dot(h_ref[...], w_ref[:, c * RET_KEY_DIM:(c + 1) * RET_KEY_DIM], preferred_element_type=f32)

    @pl.when(j < 2)
    def _():
        cos = cos_ref[...]
        sin = sin_ref[...]
        scale = jnp.where(j == 1, RET_KEY_DIM ** -0.5, 1.0)
        half = RET_KEY_DIM // 2
        for c in range(RET_TN // RET_KEY_DIM):
            proj = slab(c)
            x1, x2 = proj[:, :half], proj[:, half:]
            o_ref[:, c * RET_KEY_DIM:c * RET_KEY_DIM + half] = ((x1 * cos - x2 * sin) * scale).astype(bf16)
            o_ref[:, c * RET_KEY_DIM + half:(c + 1) * RET_KEY_DIM] = ((x2 * cos + x1 * sin) * scale).astype(bf16)

    @pl.when(j >= 2)
    def _():
        for c in range(RET_TN // RET_KEY_DIM):
            o_ref[:, c * RET_KEY_DIM:(c + 1) * RET_KEY_DIM] = slab(c).astype(bf16)


def _ret_proj(x2d, gain, w_bf16, cos, sin):
    nt = TOKENS // PROJ_TM
    nj = RET_IN_WIDTH // RET_TN
    return pl.pallas_call(
        _ret_proj_kernel,
        grid=(nt, nj),
        in_specs=[
            pl.BlockSpec((PROJ_TM, D_MODEL), lambda t, j: (t, 0)),
            pl.BlockSpec((1, D_MODEL), lambda t, j: (0, 0)),
            pl.BlockSpec((D_MODEL, RET_TN), lambda t, j: (0, j)),
            pl.BlockSpec((PROJ_TM, LANES), lambda t, j: (t, 0)),
            pl.BlockSpec((PROJ_TM, LANES), lambda t, j: (t, 0)),
        ],
        out_specs=pl.BlockSpec((PROJ_TM, RET_TN), lambda t, j: (t, j)),
        out_shape=jax.ShapeDtypeStruct((TOKENS, RET_IN_WIDTH), bf16),
        scratch_shapes=[pltpu.VMEM((PROJ_TM, D_MODEL), bf16)],
        compiler_params=_params("parallel", "arbitrary"),
        name="ret_proj",
    )(x2d, gain.reshape(1, D_MODEL), w_bf16, cos, sin)


def _ret_core_kernel(cd_ref, q_ref, k_ref, v_ref, g_ref, din_ref, qdec_ref, kdec_ref, gn_ref, y_ref, r_ref):
    h = pl.program_id(1)
    t = pl.program_id(2)
    C = RET_CHUNK

    @pl.when(t == 0)
    def _():
        r_ref[...] = jnp.zeros_like(r_ref)

    decay_in = din_ref[...]
    qdec = qdec_ref[...]
    kdec = kdec_ref[...]
    cd = cd_ref[h]
    gn = gn_ref[...]
    for c in range(RET_TQ // C):
        sl = slice(c * C, (c + 1) * C)
        qn = q_ref[sl, :]
        kn = k_ref[sl, :]
        vn = v_ref[sl, :]
        sc = lax.dot_general(qn, kn, NT_DIMS, preferred_element_type=f32) * decay_in
        inner = jnp.dot(sc.astype(bf16), vn, preferred_element_type=f32)
        r_old = r_ref[...]
        qd = (qn.astype(f32) * qdec).astype(bf16)
        cross = jnp.dot(qd, r_old.astype(bf16), preferred_element_type=f32)
        kdt = (kn.astype(f32) * kdec).T.astype(bf16)
        r_ref[...] = cd * r_old + jnp.dot(kdt, vn, preferred_element_type=f32)
        o = inner + cross
        mu = jnp.mean(o, axis=-1, keepdims=True)
        d = o - mu
        var = jnp.mean(d * d, axis=-1, keepdims=True)
        on = d * lax.rsqrt(var + EPS) * gn
        y_ref[sl, :] = (_silu(g_ref[sl, :].astype(f32)) * on).astype(bf16)


def _ret_core(proj3d, gn_g, decay_in, q_decay, k_decay, chunk_decay):
    nt = SEQ // RET_TQ
    H, dk, dv, C = RET_HEADS, RET_KEY_DIM, RET_VAL_DIM, RET_CHUNK
    v_blk0 = 2 * D_MODEL // dv
    g_blk0 = (2 * D_MODEL + H * dv) // dv
    return pl.pallas_call(
        _ret_core_kernel,
        grid=(BATCH, H, nt),
        in_specs=[
            pl.BlockSpec(memory_space=pltpu.SMEM),
            pl.BlockSpec((None, RET_TQ, dk), lambda b, h, t: (b, t, h)),
            pl.BlockSpec((None, RET_TQ, dk), lambda b, h, t: (b, t, H + h)),
            pl.BlockSpec((None, RET_TQ, dv), lambda b, h, t: (b, t, v_blk0 + h)),
            pl.BlockSpec((None, RET_TQ, dv), lambda b, h, t: (b, t, g_blk0 + h)),
            pl.BlockSpec((None, C, C), lambda b, h, t: (h, 0, 0)),
            pl.BlockSpec((None, C, 1), lambda b, h, t: (h, 0, 0)),
            pl.BlockSpec((None, C, 1), lambda b, h, t: (h, 0, 0)),
            pl.BlockSpec((1, dv), lambda b, h, t: (0, h)),
        ],
        out_specs=pl.BlockSpec((None, RET_TQ, dv), lambda b, h, t: (b, t, h)),
        out_shape=jax.ShapeDtypeStruct((BATCH, SEQ, H * dv), bf16),
        scratch_shapes=[pltpu.VMEM((dk, dv), f32)],
        compiler_params=_params("parallel", "parallel", "arbitrary"),
        name="ret_core",
    )(chunk_decay, proj3d, proj3d, proj3d, proj3d, decay_in, q_decay, k_decay, gn_g.reshape(1, H * dv))


def _ffn_kernel(y_ref, wo_ref, x_ref, g_ref, wa_ref, wb_ref, cw_ref, cb_ref, wd_ref, fg_ref, o_ref, aprev_ref,
                *, final_norm):
    t = pl.program_id(0)

    @pl.when(t == 0)
    def _():
        aprev_ref[...] = jnp.zeros_like(aprev_ref)

    x1 = x_ref[...] + jnp.dot(y_ref[...], wo_ref[...], preferred_element_type=f32)
    h = _rms_norm(x1, g_ref[...]).astype(bf16)
    seq_start = (t * FFN_TM) % SEQ == 0
    acc = x1
    for c in range(D_FF // FFN_CHUNK):
        cs = slice(c * FFN_CHUNK, (c + 1) * FFN_CHUNK)
        a = jnp.dot(h, wa_ref[:, cs], preferred_element_type=f32)
        b = jnp.dot(h, wb_ref[:, cs], preferred_element_type=f32)
        prev = jnp.where(seq_start, 0.0, aprev_ref[:, cs])
        aprev_ref[:, cs] = a[FFN_TM - FFN_HALO:, :]
        a_full = jnp.concatenate([prev, a], axis=0)
        a1 = pltpu.roll(a_full, 1, 0)[FFN_HALO:]
        a2 = pltpu.roll(a_full, 2, 0)[FFN_HALO:]
        cw = cw_ref[:, cs]
        conv = cw[0:1] * a2 + cw[1:2] * a1 + cw[2:3] * a + cb_ref[:, cs]
        hid = (_silu(conv) * b).astype(bf16)
        acc = acc + jnp.dot(hid, wd_ref[cs, :], preferred_element_type=f32)
    o_ref[...] = _rms_norm(acc, fg_ref[...]) if final_norm else acc


def _ffn(y2d, w_out, x2d, gain, wa, wb, conv_w, conv_b, wd, final_gain):
    kdim = y2d.shape[1]
    nt = TOKENS // FFN_TM
    resident = dict(pipeline_mode=pl.Buffered(1))
    final_norm = final_gain is not None
    fg = final_gain if final_norm else gain
    return pl.pallas_call(
        functools.partial(_ffn_kernel, final_norm=final_norm),
        grid=(nt,),
        in_specs=[
            pl.BlockSpec((FFN_TM, kdim), lambda t: (t, 0)),
            pl.BlockSpec((kdim, D_MODEL), lambda t: (0, 0), **resident),
            pl.BlockSpec((FFN_TM, D_MODEL), lambda t: (t, 0)),
            pl.BlockSpec((1, D_MODEL), lambda t: (0, 0)),
            pl.BlockSpec((D_MODEL, D_FF), lambda t: (0, 0), **resident),
            pl.BlockSpec((D_MODEL, D_FF), lambda t: (0, 0), **resident),
            pl.BlockSpec((CONV_WIDTH, D_FF), lambda t: (0, 0)),
            pl.BlockSpec((1, D_FF), lambda t: (0, 0)),
            pl.BlockSpec((D_FF, D_MODEL), lambda t: (0, 0), **resident),
            pl.BlockSpec((1, D_MODEL), lambda t: (0, 0)),
        ],
        out_specs=pl.BlockSpec((FFN_TM, D_MODEL), lambda t: (t, 0)),
        out_shape=jax.ShapeDtypeStruct((TOKENS, D_MODEL), f32),
        scratch_shapes=[pltpu.VMEM((FFN_HALO, D_FF), f32)],
        compiler_params=_params("arbitrary"),
        name="out_proj_conv_ffn",
    )(y2d, w_out, x2d, gain.reshape(1, D_MODEL), wa, wb, conv_w, conv_b.reshape(1, D_FF), wd, fg.reshape(1, D_MODEL))


def _rope_angles(positions, dim):
    inv = 1.0 / (ROPE_THETA ** (jnp.arange(0, dim, 2, dtype=f32) / dim))
    return positions.astype(f32)[..., None] * inv


def _head_rope_tables(positions):
    ang = _rope_angles(positions, HEAD_DIM)
    cos, sin = jnp.cos(ang), jnp.sin(ang)
    reps = LANES // HEAD_DIM
    cosf = jnp.tile(jnp.concatenate([cos, cos], axis=-1), (1, 1, reps))
    sins = jnp.tile(jnp.concatenate([-sin, sin], axis=-1), (1, 1, reps))
    return cosf, sins


def _retention_decays():
    H, C = RET_HEADS, RET_CHUNK
    log_gamma = jnp.log(1.0 - 2.0 ** (-5.0 - jnp.arange(H, dtype=f32)))
    idx = jnp.arange(C, dtype=f32)
    diff = idx[:, None] - idx[None, :]
    decay_in = jnp.where(diff[None] >= 0, jnp.exp(jnp.maximum(diff, 0.0)[None] * log_gamma[:, None, None]), 0.0)
    q_decay = jnp.exp((idx + 1.0)[None, :] * log_gamma[:, None])
    k_decay = jnp.exp((C - 1.0 - idx)[None, :] * log_gamma[:, None])
    chunk_decay = jnp.exp(C * log_gamma)
    return decay_in, q_decay[..., None], k_decay[..., None], chunk_decay


def kernel(x, positions, l0_attn_norm, l0_w_in, l0_w_out, l0_sinks, l0_ffn_norm, l0_w_a, l0_w_b, l0_conv_w, l0_conv_b, l0_w_down, l1_attn_norm, l1_w_in, l1_w_out, l1_gn_g, l1_ffn_norm, l1_w_a, l1_w_b, l1_conv_w, l1_conv_b, l1_w_down, l2_attn_norm, l2_w_in, l2_w_out, l2_ffn_norm, l2_w_a, l2_w_b, l2_conv_w, l2_conv_b, l2_w_down, l3_attn_norm, l3_w_in, l3_w_out, l3_sinks, l3_ffn_norm, l3_w_a, l3_w_b, l3_conv_w, l3_conv_b, l3_w_down, final_norm):
    mixers = [
        (l0_attn_norm, l0_w_in, l0_w_out, l0_sinks),
        (l1_attn_norm, l1_w_in, l1_w_out, l1_gn_g),
        (l2_attn_norm, l2_w_in, l2_w_out, None),
        (l3_attn_norm, l3_w_in, l3_w_out, l3_sinks),
    ]
    ffns = [
        (l0_ffn_norm, l0_w_a, l0_w_b, l0_conv_w, l0_conv_b, l0_w_down),
        (l1_ffn_norm, l1_w_a, l1_w_b, l1_conv_w, l1_conv_b, l1_w_down),
        (l2_ffn_norm, l2_w_a, l2_w_b, l2_conv_w, l2_conv_b, l2_w_down),
        (l3_ffn_norm, l3_w_a, l3_w_b, l3_conv_w, l3_conv_b, l3_w_down),
    ]
    cosf, sins = _head_rope_tables(positions)
    ang_r = _rope_angles(positions, RET_KEY_DIM).reshape(TOKENS, RET_KEY_DIM // 2)
    cos_r, sin_r = jnp.cos(ang_r), jnp.sin(ang_r)
    decay_in, q_decay, k_decay, chunk_decay = _retention_decays()

    x2d = x.reshape(TOKENS, D_MODEL)
    for i in range(DEPTH):
        norm_g, w_in, w_out, extra = mixers[i]
        m = i % N_MIXERS
        w_in_b = w_in.astype(bf16)
        if m == 1:
            proj = _ret_proj(x2d, norm_g, w_in_b, cos_r, sin_r)
            y = _ret_core(proj.reshape(BATCH, SEQ, RET_IN_WIDTH), extra, decay_in, q_decay, k_decay, chunk_decay)
            y2d = y.reshape(TOKENS, RET_HEADS * RET_VAL_DIM)
        else:
            q, k, v = _attn_proj(x2d.reshape(BATCH, SEQ, D_MODEL), norm_g, w_in_b, cosf, sins)
            y = _swa(q, k, v, extra) if m == 0 else _moba(q, k, v)
            y2d = y.reshape(TOKENS, D_MODEL)
        fg, wa, wb, cw, cb, wd = ffns[i]
        x2d = _ffn(y2d, w_out.astype(bf16), x2d, fg, wa.astype(bf16), wb.astype(bf16), cw, cb, wd.astype(bf16),
                   final_norm if i == DEPTH - 1 else None)
    return x2d.reshape(BATCH, SEQ, D_MODEL)
```

```python
import functools

import jax
import jax.numpy as jnp
from jax import lax
from jax.experimental import pallas as pl
from jax.experimental.pallas import tpu as pltpu

D_MODEL = 1024
BATCH = 4
SEQ = 4096
DEPTH = 4
N_MIXERS = 3
HEAD_DIM = 64
N_Q_HEADS = D_MODEL // HEAD_DIM
N_KV_HEADS = 4
GQA_GROUP = N_Q_HEADS // N_KV_HEADS
QKV_WIDTH = (N_Q_HEADS + 2 * N_KV_HEADS) * HEAD_DIM
ROPE_THETA = 10000.0
SWA_WINDOW = 128
RET_HEADS = 4
RET_KEY_DIM = D_MODEL // RET_HEADS
RET_VAL_DIM = 2 * RET_KEY_DIM
RET_IN_WIDTH = 2 * D_MODEL + 2 * RET_HEADS * RET_VAL_DIM
RET_CHUNK = 128
MOBA_BLOCK = 256
MOBA_TOPK = 3
D_FF = ((8 * D_MODEL // 3 + 127) // 128) * 128
CONV_WIDTH = 3
EPS = 1e-6
NEG_INF = -1e30

LANES = 128
SUBLANES = 8
MXU_WIDTH = 256
VMEM_LIMIT_BYTES = 56 * 1024 * 1024

TOKENS = BATCH * SEQ
PROJ_TM = 512
ATT_TQ = 512
RET_TQ = 512
MOBA_ROW_CHUNK = 128
MOBA_GROUPS = 4
FFN_TM = 512
FFN_CHUNK = 256
FFN_HALO = SUBLANES

f32 = jnp.float32
bf16 = jnp.bfloat16
NT_DIMS = (((1,), (1,)), ((), ()))


def _params(*semantics):
    return pltpu.CompilerParams(dimension_semantics=semantics, vmem_limit_bytes=VMEM_LIMIT_BYTES)


def _rms_norm(x, gain):
    ms = jnp.mean(x * x, axis=-1, keepdims=True)
    return x * lax.rsqrt(ms + EPS) * gain


def _silu(x):
    return x * (0.5 + 0.5 * jnp.tanh(0.5 * x))


def _normalize_pair(pv_even, pv_odd, extra_even, extra_odd):
    low = lax.broadcasted_iota(jnp.int32, pv_even.shape, 1) < HEAD_DIM
    num = jnp.where(low, pv_even, pltpu.roll(pv_odd, HEAD_DIM, 1))
    den = jnp.where(low, pltpu.roll(pv_even, HEAD_DIM, 1) + extra_even, pv_odd + extra_odd)
    return num / den


def _attn_proj_kernel(x_ref, g_ref, w_ref, cos_ref, sin_ref, q_ref, k_ref, v_ref):
    t = pl.program_id(1)
    h = _rms_norm(x_ref[...], g_ref[...]).astype(bf16)
    cosf = cos_ref[...]
    sins = sin_ref[...]
    lane = lax.broadcasted_iota(jnp.int32, (PROJ_TM, LANES), 1)
    first_half = (lane % HEAD_DIM) < (HEAD_DIM // 2)
    low = lane < HEAD_DIM
    blk = (t * PROJ_TM + lax.broadcasted_iota(jnp.int32, (PROJ_TM, LANES), 0)) // MOBA_BLOCK
    k_tail = jnp.where(lane == HEAD_DIM + blk, 1.0, 0.0)
    v_tail = 1.0

    def split(r, tail):
        return (jnp.where(low, r, tail).astype(bf16),
                jnp.where(low, pltpu.roll(r, HEAD_DIM, 1), tail).astype(bf16))

    def rope(xc):
        partner = jnp.where(first_half, pltpu.roll(xc, LANES - HEAD_DIM // 2, 1), pltpu.roll(xc, HEAD_DIM // 2, 1))
        return xc * cosf + partner * sins

    n_q = N_Q_HEADS * HEAD_DIM // LANES
    n_k = N_KV_HEADS * HEAD_DIM // LANES
    groups_per_dot = MXU_WIDTH // LANES
    for d in range(QKV_WIDTH // MXU_WIDTH):
        proj = jnp.dot(h, w_ref[:, d * MXU_WIDTH:(d + 1) * MXU_WIDTH], preferred_element_type=f32)
        for e in range(groups_per_dot):
            c = d * groups_per_dot + e
            xc = proj[:, e * LANES:(e + 1) * LANES]
            if c < n_q:
                q_ref[2 * c], q_ref[2 * c + 1] = split(rope(xc) * (HEAD_DIM ** -0.5), 0.0)
            elif c < n_q + n_k:
                k_ref[2 * (c - n_q)], k_ref[2 * (c - n_q) + 1] = split(rope(xc), k_tail)
            else:
                v_ref[2 * (c - n_q - n_k)], v_ref[2 * (c - n_q - n_k) + 1] = split(xc, v_tail)


def _attn_proj(x, gain, w_bf16, cosf, sins):
    nt = SEQ // PROJ_TM
    return pl.pallas_call(
        _attn_proj_kernel,
        grid=(BATCH, nt),
        in_specs=[
            pl.BlockSpec((None, PROJ_TM, D_MODEL), lambda b, t: (b, t, 0)),
            pl.BlockSpec((1, D_MODEL), lambda b, t: (0, 0)),
            pl.BlockSpec((D_MODEL, QKV_WIDTH), lambda b, t: (0, 0)),
            pl.BlockSpec((None, PROJ_TM, LANES), lambda b, t: (b, t, 0)),
            pl.BlockSpec((None, PROJ_TM, LANES), lambda b, t: (b, t, 0)),
        ],
        out_specs=[
            pl.BlockSpec((None, N_Q_HEADS, PROJ_TM, LANES), lambda b, t: (b, 0, t, 0)),
            pl.BlockSpec((None, N_KV_HEADS, PROJ_TM, LANES), lambda b, t: (b, 0, t, 0)),
            pl.BlockSpec((None, N_KV_HEADS, PROJ_TM, LANES), lambda b, t: (b, 0, t, 0)),
        ],
        out_shape=[
            jax.ShapeDtypeStruct((BATCH, N_Q_HEADS, SEQ, LANES), bf16),
            jax.ShapeDtypeStruct((BATCH, N_KV_HEADS, SEQ, LANES), bf16),
            jax.ShapeDtypeStruct((BATCH, N_KV_HEADS, SEQ, LANES), bf16),
        ],
        compiler_params=_params("parallel", "parallel"),
        name="attn_proj",
    )(x, gain.reshape(1, D_MODEL), w_bf16, cosf, sins)


def _swa_kernel(sinks_ref, q_ref, k_ref, v_ref, o_ref, s_ref):
    g = pl.program_id(1)
    t = pl.program_id(2)
    W = SWA_WINDOW
    nsub = ATT_TQ // W
    qi = lax.broadcasted_iota(jnp.int32, (W, 2 * W), 0)
    col = lax.broadcasted_iota(jnp.int32, (W, 2 * W), 1)
    band = jnp.where((col > qi) & (col <= qi + W), 0.0, NEG_INF)
    first = jnp.where(col <= qi, 0.0, NEG_INF)

    def key_start(c):
        return pl.multiple_of(jnp.maximum(t * ATT_TQ + (c - 1) * W, 0), W)

    def scores(c):
        q4 = q_ref[:, c * W:(c + 1) * W, :].reshape(GQA_GROUP * W, LANES)
        kk = k_ref[pl.ds(key_start(c), 2 * W), :]
        s_ref[c] = lax.dot_general(q4, kk, NT_DIMS, preferred_element_type=f32)

    def head(c, j, vv, bias):
        sink = sinks_ref[g * GQA_GROUP + j]
        s = s_ref[c, j * W:(j + 1) * W, :] + bias
        s0, s1 = s[:, :LANES], s[:, LANES:]
        m = jnp.maximum(jnp.max(jnp.maximum(s0, s1), axis=1, keepdims=True), sink)
        p = jnp.concatenate([jnp.exp(s0 - m), jnp.exp(s1 - m)], axis=1).astype(bf16)
        return jnp.dot(p, vv, preferred_element_type=f32), jnp.exp(sink - m)

    scores(0)
    for c in range(nsub):
        if c + 1 < nsub:
            scores(c + 1)
        vv = v_ref[pl.ds(key_start(c), 2 * W), :]
        bias = jnp.where(t == 0, first, band) if c == 0 else band
        for jj in range(GQA_GROUP // 2):
            pv0, e0 = head(c, 2 * jj, vv, bias)
            pv1, e1 = head(c, 2 * jj + 1, vv, bias)
            o_ref[c * W:(c + 1) * W, jj * LANES:(jj + 1) * LANES] = _normalize_pair(pv0, pv1, e0, e1).astype(bf16)


def _swa(q, k, v, sinks):
    nt = SEQ // ATT_TQ
    gw = GQA_GROUP * HEAD_DIM
    return pl.pallas_call(
        _swa_kernel,
        grid=(BATCH, N_KV_HEADS, nt),
        in_specs=[
            pl.BlockSpec(memory_space=pltpu.SMEM),
            pl.BlockSpec((None, GQA_GROUP, ATT_TQ, LANES), lambda b, g, t: (b, g, t, 0)),
            pl.BlockSpec((None, None, SEQ, LANES), lambda b, g, t: (b, g, 0, 0)),
            pl.BlockSpec((None, None, SEQ, LANES), lambda b, g, t: (b, g, 0, 0)),
        ],
        out_specs=pl.BlockSpec((None, ATT_TQ, gw), lambda b, g, t: (b, t, g)),
        out_shape=jax.ShapeDtypeStruct((BATCH, SEQ, D_MODEL), bf16),
        scratch_shapes=[pltpu.VMEM((ATT_TQ // SWA_WINDOW, GQA_GROUP * SWA_WINDOW, 2 * SWA_WINDOW), f32)],
        compiler_params=_params("parallel", "parallel", "parallel"),
        name="swa_core",
    )(sinks, q, k, v)


def _moba_kernel(q_ref, k_ref, v_ref, o_ref, kmean_ref, qa_ref, sa_ref, sb_ref, m_ref, acc_ref):
    i = pl.program_id(2)
    BS = MOBA_BLOCK
    nb = SEQ // BS
    G = MOBA_GROUPS
    grows = GQA_GROUP * BS
    rows = G * grows

    @pl.when(i == 0)
    def _():
        lane = lax.broadcasted_iota(jnp.int32, (1, LANES), 1)
        for g in range(G):
            for n in range(nb):
                kb = k_ref[g, n * BS:(n + 1) * BS, :].astype(f32)
                kmean_ref[g, n:n + 1, :] = jnp.where(lane < HEAD_DIM, jnp.mean(kb, axis=0, keepdims=True), 0.0)

    q = q_ref[...].reshape(rows, LANES)
    gate_t = jnp.concatenate(
        [lax.dot_general(kmean_ref[g].astype(bf16), q[g * grows:(g + 1) * grows], NT_DIMS,
                         preferred_element_type=f32) for g in range(G)], axis=1)
    blk = lax.broadcasted_iota(jnp.int32, (nb, rows), 0)
    past = blk < i
    gm = jnp.where(past, gate_t, NEG_INF)
    sel = jnp.zeros((nb, rows), jnp.int32)
    for _ in range(MOBA_TOPK):
        mx = jnp.max(gm, axis=0, keepdims=True)
        cand = (gm == mx) & past & (sel == 0)
        idx = jnp.min(jnp.where(cand, blk, nb), axis=0, keepdims=True)
        pick = blk == idx
        sel = jnp.where(pick, 1, sel)
        gm = jnp.where(pick, NEG_INF, gm)
    bias_t = jnp.where((sel == 1) | jnp.logical_not(past), 0.0, NEG_INF)
    bias_pad_t = jnp.concatenate(
        [jnp.zeros((HEAD_DIM, rows), f32), bias_t, jnp.zeros((LANES - HEAD_DIM - nb, rows), f32)], axis=0)
    qa_ref[...] = (q.astype(f32) + bias_pad_t.T).astype(bf16)

    RC = MOBA_ROW_CHUNK
    qi = lax.broadcasted_iota(jnp.int32, (RC, LANES), 0)
    kj = lax.broadcasted_iota(jnp.int32, (RC, LANES), 1)

    def block_start(n):
        return pl.multiple_of(jnp.minimum(n, nb - 1) * BS, BS)

    def scores(n, s_ref):
        for g in range(G):
            gs = slice(g * grows, (g + 1) * grows)
            k_n = k_ref[g, pl.ds(block_start(n), BS), :]
            s_ref[gs, :] = lax.dot_general(qa_ref[gs, :], k_n, NT_DIMS, preferred_element_type=f32)

    def softmax_pv(n, s_ref, own):
        for g in range(G):
            v_n = v_ref[g, pl.ds(block_start(n), BS), :]
            for c in range(grows // RC):
                rs = slice(g * grows + c * RC, g * grows + (c + 1) * RC)
                s = s_ref[rs, :]
                s0, s1 = s[:, :LANES], s[:, LANES:]
                if own:
                    q0 = (c * RC) % BS
                    s0 = jnp.where(kj <= qi + q0, s0, NEG_INF)
                    s1 = jnp.where(kj + LANES <= qi + q0, s1, NEG_INF)
                row_max = jnp.max(jnp.maximum(s0, s1), axis=1, keepdims=True)
                if own:
                    m_new = jnp.broadcast_to(row_max, (RC, LANES))
                else:
                    m_old = m_ref[rs, :]
                    m_new = jnp.maximum(m_old, row_max)
                p = jnp.concatenate([jnp.exp(s0 - m_new), jnp.exp(s1 - m_new)], axis=1).astype(bf16)
                pv = jnp.dot(p, v_n, preferred_element_type=f32)
                if own:
                    acc_ref[rs, :] = pv
                else:
                    acc_ref[rs, :] = jnp.exp(m_old - m_new) * acc_ref[rs, :] + pv
                m_ref[rs, :] = m_new

    scores(i, sa_ref)
    scores(0, sb_ref)
    softmax_pv(i, sa_ref, True)

    def body(n, carry):
        @pl.when(n % 2 == 0)
        def _():
            scores(n + 1, sa_ref)
            softmax_pv(n, sb_ref, False)

        @pl.when(n % 2 == 1)
        def _():
            scores(n + 1, sb_ref)
            softmax_pv(n, sa_ref, False)

        return carry

    lax.fori_loop(0, i, body, 0)
    for jj in range(G * GQA_GROUP // 2):
        a0 = acc_ref[2 * jj * BS:(2 * jj + 1) * BS, :]
        a1 = acc_ref[(2 * jj + 1) * BS:(2 * jj + 2) * BS, :]
        o_ref[:, jj * LANES:(jj + 1) * LANES] = _normalize_pair(a0, a1, 0.0, 0.0).astype(bf16)


def _moba(q, k, v):
    nb = SEQ // MOBA_BLOCK
    G = MOBA_GROUPS
    gw = G * GQA_GROUP * HEAD_DIM
    rows = G * GQA_GROUP * MOBA_BLOCK
    return pl.pallas_call(
        _moba_kernel,
        grid=(BATCH, N_KV_HEADS // G, nb),
        in_specs=[
            pl.BlockSpec((None, G * GQA_GROUP, MOBA_BLOCK, LANES), lambda b, g, i: (b, g, i, 0)),
            pl.BlockSpec((None, G, SEQ, LANES), lambda b, g, i: (b, g, 0, 0)),
            pl.BlockSpec((None, G, SEQ, LANES), lambda b, g, i: (b, g, 0, 0)),
        ],
        out_specs=pl.BlockSpec((None, MOBA_BLOCK, gw), lambda b, g, i: (b, i, g)),
        out_shape=jax.ShapeDtypeStruct((BATCH, SEQ, D_MODEL), bf16),
        scratch_shapes=[
            pltpu.VMEM((G, nb, LANES), f32),
            pltpu.VMEM((rows, LANES), bf16),
            pltpu.VMEM((rows, MOBA_BLOCK), f32),
            pltpu.VMEM((rows, MOBA_BLOCK), f32),
            pltpu.VMEM((rows, LANES), f32),
            pltpu.VMEM((rows, LANES), f32),
        ],
        compiler_params=_params("parallel", "parallel", "arbitrary"),
        name="moba_core",
    )(q, k, v)


def _ret_proj_kernel(x_ref, g_ref, w_ref, cos_ref, sin_ref, o_ref):
    h = _rms_norm(x_ref[...], g_ref[...]).astype(bf16)
    cos = cos_ref[...]
    sin = sin_ref[...]
    half = RET_KEY_DIM // 2
    n_rope = 2 * RET_HEADS
    for c in range(RET_IN_WIDTH // RET_KEY_DIM):
        c0 = c * RET_KEY_DIM
        proj = jnp.dot(h, w_ref[:, c0:c0 + RET_KEY_DIM], preferred_element_type=f32)
        if c < n_rope:
            x1, x2 = proj[:, :half], proj[:, half:]
            r1, r2 = x1 * cos - x2 * sin, x2 * cos + x1 * sin
            if c >= RET_HEADS:
                r1, r2 = r1 * (RET_KEY_DIM ** -0.5), r2 * (RET_KEY_DIM ** -0.5)
            o_ref[:, c0:c0 + half] = r1.astype(bf16)
            o_ref[:, c0 + half:c0 + RET_KEY_DIM] = r2.astype(bf16)
        else:
            o_ref[:, c0:c0 + RET_KEY_DIM] = proj.astype(bf16)


def _ret_proj(x2d, gain, w_bf16, cos, sin):
    nt = TOKENS // PROJ_TM
    return pl.pallas_call(
        _ret_proj_kernel,
        grid=(nt,),
        in_specs=[
            pl.BlockSpec((PROJ_TM, D_MODEL), lambda t: (t, 0)),
            pl.BlockSpec((1, D_MODEL), lambda t: (0, 0)),
            pl.BlockSpec((D_MODEL, RET_IN_WIDTH), lambda t: (0, 0), pipeline_mode=pl.Buffered(1)),
            pl.BlockSpec((PROJ_TM, LANES), lambda t: (t, 0)),
            pl.BlockSpec((PROJ_TM, LANES), lambda t: (t, 0)),
        ],
        out_specs=pl.BlockSpec((PROJ_TM, RET_IN_WIDTH), lambda t: (t, 0)),
        out_shape=jax.ShapeDtypeStruct((TOKENS, RET_IN_WIDTH), bf16),
        compiler_params=_params("parallel"),
        name="ret_proj",
    )(x2d, gain.reshape(1, D_MODEL), w_bf16, cos, sin)


def _ret_core_kernel(cd_ref, q_ref, k_ref, v_ref, g_ref, din_ref, qdec_ref, kdec_ref, gn_ref, y_ref, r_ref):
    h = pl.program_id(1)
    t = pl.program_id(2)
    C = RET_CHUNK

    @pl.when(t == 0)
    def _():
        r_ref[...] = jnp.zeros_like(r_ref)

    decay_in = din_ref[...]
    qdec = qdec_ref[...]
    kdec = kdec_ref[...]
    cd = cd_ref[h]
    gn = gn_ref[...]
    for c in range(RET_TQ // C):
        sl = slice(c * C, (c + 1) * C)
        qn = q_ref[sl, :]
        kn = k_ref[sl, :]
        vn = v_ref[sl, :]
        sc = lax.dot_general(qn, kn, NT_DIMS, preferred_element_type=f32) * decay_in
        inner = jnp.dot(sc.astype(bf16), vn, preferred_element_type=f32)
        r_old = r_ref[...]
        qd = (qn.astype(f32) * qdec).astype(bf16)
        cross = jnp.dot(qd, r_old.astype(bf16), preferred_element_type=f32)
        kdt = (kn.astype(f32) * kdec).T.astype(bf16)
        r_ref[...] = cd * r_old + jnp.dot(kdt, vn, preferred_element_type=f32)
        o = inner + cross
        mu = jnp.mean(o, axis=-1, keepdims=True)
        d = o - mu
        var = jnp.mean(d * d, axis=-1, keepdims=True)
        on = d * lax.rsqrt(var + EPS) * gn
        y_ref[sl, :] = (_silu(g_ref[sl, :].astype(f32)) * on).astype(bf16)


def _ret_core(proj3d, gn_g, decay_in, q_decay, k_decay, chunk_decay):
    nt = SEQ // RET_TQ
    H, dk, dv, C = RET_HEADS, RET_KEY_DIM, RET_VAL_DIM, RET_CHUNK
    v_blk0 = 2 * D_MODEL // dv
    g_blk0 = (2 * D_MODEL + H * dv) // dv
    return pl.pallas_call(
        _ret_core_kernel,
        grid=(BATCH, H, nt),
        in_specs=[
            pl.BlockSpec(memory_space=pltpu.SMEM),
            pl.BlockSpec((None, RET_TQ, dk), lambda b, h, t: (b, t, h)),
            pl.BlockSpec((None, RET_TQ, dk), lambda b, h, t: (b, t, H + h)),
            pl.BlockSpec((None, RET_TQ, dv), lambda b, h, t: (b, t, v_blk0 + h)),
            pl.BlockSpec((None, RET_TQ, dv), lambda b, h, t: (b, t, g_blk0 + h)),
            pl.BlockSpec((None, C, C), lambda b, h, t: (h, 0, 0)),
            pl.BlockSpec((None, C, 1), lambda b, h, t: (h, 0, 0)),
            pl.BlockSpec((None, C, 1), lambda b, h, t: (h, 0, 0)),
            pl.BlockSpec((1, dv), lambda b, h, t: (0, h)),
        ],
        out_specs=pl.BlockSpec((None, RET_TQ, dv), lambda b, h, t: (b, t, h)),
        out_shape=jax.ShapeDtypeStruct((BATCH, SEQ, H * dv), bf16),
        scratch_shapes=[pltpu.VMEM((dk, dv), f32)],
        compiler_params=_params("parallel", "parallel", "arbitrary"),
        name="ret_core",
    )(chunk_decay, proj3d, proj3d, proj3d, proj3d, decay_in, q_decay, k_decay, gn_g.reshape(1, H * dv))


def _ffn_kernel(y_ref, wo_ref, x_ref, g_ref, wa_ref, wb_ref, cw_ref, cb_ref, wd_ref, fg_ref, o_ref, aprev_ref,
                *, final_norm):
    t = pl.program_id(0)

    @pl.when(t == 0)
    def _():
        aprev_ref[...] = jnp.zeros_like(aprev_ref)

    x1 = x_ref[...] + jnp.dot(y_ref[...], wo_ref[...], preferred_element_type=f32)
    h = _rms_norm(x1, g_ref[...]).astype(bf16)
    seq_start = (t * FFN_TM) % SEQ == 0
    acc = x1
    for c in range(D_FF // FFN_CHUNK):
        cs = slice(c * FFN_CHUNK, (c + 1) * FFN_CHUNK)
        a = jnp.dot(h, wa_ref[:, cs], preferred_element_type=f32)
        b = jnp.dot(h, wb_ref[:, cs], preferred_element_type=f32)
        prev = jnp.where(seq_start, 0.0, aprev_ref[:, cs])
        aprev_ref[:, cs] = a[FFN_TM - FFN_HALO:, :]
        a_full = jnp.concatenate([prev, a], axis=0)
        a1 = pltpu.roll(a_full, 1, 0)[FFN_HALO:]
        a2 = pltpu.roll(a_full, 2, 0)[FFN_HALO:]
        cw = cw_ref[:, cs]
        conv = cw[0:1] * a2 + cw[1:2] * a1 + cw[2:3] * a + cb_ref[:, cs]
        hid = (_silu(conv) * b).astype(bf16)
        acc = acc + jnp.dot(hid, wd_ref[cs, :], preferred_element_type=f32)
    o_ref[...] = _rms_norm(acc, fg_ref[...]) if final_norm else acc


def _ffn(y2d, w_out, x2d, gain, wa, wb, conv_w, conv_b, wd, final_gain):
    kdim = y2d.shape[1]
    nt = TOKENS // FFN_TM
    resident = dict(pipeline_mode=pl.Buffered(1))
    final_norm = final_gain is not None
    fg = final_gain if final_norm else gain
    return pl.pallas_call(
        functools.partial(_ffn_kernel, final_norm=final_norm),
        grid=(nt,),
        in_specs=[
            pl.BlockSpec((FFN_TM, kdim), lambda t: (t, 0)),
            pl.BlockSpec((kdim, D_MODEL), lambda t: (0, 0), **resident),
            pl.BlockSpec((FFN_TM, D_MODEL), lambda t: (t, 0)),
            pl.BlockSpec((1, D_MODEL), lambda t: (0, 0)),
            pl.BlockSpec((D_MODEL, D_FF), lambda t: (0, 0), **resident),
            pl.BlockSpec((D_MODEL, D_FF), lambda t: (0, 0), **resident),
            pl.BlockSpec((CONV_WIDTH, D_FF), lambda t: (0, 0)),
            pl.BlockSpec((1, D_FF), lambda t: (0, 0)),
            pl.BlockSpec((D_FF, D_MODEL), lambda t: (0, 0), **resident),
            pl.BlockSpec((1, D_MODEL), lambda t: (0, 0)),
        ],
        out_specs=pl.BlockSpec((FFN_TM, D_MODEL), lambda t: (t, 0)),
        out_shape=jax.ShapeDtypeStruct((TOKENS, D_MODEL), f32),
        scratch_shapes=[pltpu.VMEM((FFN_HALO, D_FF), f32)],
        compiler_params=_params("arbitrary"),
        name="out_proj_conv_ffn",
    )(y2d, w_out, x2d, gain.reshape(1, D_MODEL), wa, wb, conv_w, conv_b.reshape(1, D_FF), wd, fg.reshape(1, D_MODEL))


def _rope_angles(positions, dim):
    inv = 1.0 / (ROPE_THETA ** (jnp.arange(0, dim, 2, dtype=f32) / dim))
    return positions.astype(f32)[..., None] * inv


def _head_rope_tables(positions):
    ang = _rope_angles(positions, HEAD_DIM)
    cos, sin = jnp.cos(ang), jnp.sin(ang)
    reps = LANES // HEAD_DIM
    cosf = jnp.tile(jnp.concatenate([cos, cos], axis=-1), (1, 1, reps))
    sins = jnp.tile(jnp.concatenate([-sin, sin], axis=-1), (1, 1, reps))
    return cosf, sins


def _retention_decays():
    H, C = RET_HEADS, RET_CHUNK
    log_gamma = jnp.log(1.0 - 2.0 ** (-5.0 - jnp.arange(H, dtype=f32)))
    idx = jnp.arange(C, dtype=f32)
    diff = idx[:, None] - idx[None, :]
    decay_in = jnp.where(diff[None] >= 0, jnp.exp(jnp.maximum(diff, 0.0)[None] * log_gamma[:, None, None]), 0.0)
    q_decay = jnp.exp((idx + 1.0)[None, :] * log_gamma[:, None])
    k_decay = jnp.exp((C - 1.0 - idx)[None, :] * log_gamma[:, None])
    chunk_decay = jnp.exp(C * log_gamma)
    return decay_in, q_decay[..., None], k_decay[..., None], chunk_decay


def kernel(x, positions, l0_attn_norm, l0_w_in, l0_w_out, l0_sinks, l0_ffn_norm, l0_w_a, l0_w_b, l0_conv_w, l0_conv_b, l0_w_down, l1_attn_norm, l1_w_in, l1_w_out, l1_gn_g, l1_ffn_norm, l1_w_a, l1_w_b, l1_conv_w, l1_conv_b, l1_w_down, l2_attn_norm, l2_w_in, l2_w_out, l2_ffn_norm, l2_w_a, l2_w_b, l2_conv_w, l2_conv_b, l2_w_down, l3_attn_norm, l3_w_in, l3_w_out, l3_sinks, l3_ffn_norm, l3_w_a, l3_w_b, l3_conv_w, l3_conv_b, l3_w_down, final_norm):
    mixers = [
        (l0_attn_norm, l0_w_in, l0_w_out, l0_sinks),
        (l1_attn_norm, l1_w_in, l1_w_out, l1_gn_g),
        (l2_attn_norm, l2_w_in, l2_w_out, None),
        (l3_attn_norm, l3_w_in, l3_w_out, l3_sinks),
    ]
    ffns = [
        (l0_ffn_norm, l0_w_a, l0_w_b, l0_conv_w, l0_conv_b, l0_w_down),
        (l1_ffn_norm, l1_w_a, l1_w_b, l1_conv_w, l1_conv_b, l1_w_down),
        (l2_ffn_norm, l2_w_a, l2_w_b, l2_conv_w, l2_conv_b, l2_w_down),
        (l3_ffn_norm, l3_w_a, l3_w_b, l3_conv_w, l3_conv_b, l3_w_down),
    ]
    cosf, sins = _head_rope_tables(positions)
    ang_r = _rope_angles(positions, RET_KEY_DIM).reshape(TOKENS, RET_KEY_DIM // 2)
    cos_r, sin_r = jnp.cos(ang_r), jnp.sin(ang_r)
    decay_in, q_decay, k_decay, chunk_decay = _retention_decays()

    x2d = x.reshape(TOKENS, D_MODEL)
    for i in range(DEPTH):
        norm_g, w_in, w_out, extra = mixers[i]
        m = i % N_MIXERS
        w_in_b = w_in.astype(bf16)
        if m == 1:
            proj = _ret_proj(x2d, norm_g, w_in_b, cos_r, sin_r)
            y = _ret_core(proj.reshape(BATCH, SEQ, RET_IN_WIDTH), extra, decay_in, q_decay, k_decay, chunk_decay)
            y2d = y.reshape(TOKENS, RET_HEADS * RET_VAL_DIM)
        else:
            q, k, v = _attn_proj(x2d.reshape(BATCH, SEQ, D_MODEL), norm_g, w_in_b, cosf, sins)
            y = _swa(q, k, v, extra) if m == 0 else _moba(q, k, v)
            y2d = y.reshape(TOKENS, D_MODEL)
        fg, wa, wb, cw, cb, wd = ffns[i]
        x2d = _ffn(y2d, w_out.astype(bf16), x2d, fg, wa.astype(bf16), wb.astype(bf16), cw, cb, wd.astype(bf16),
                   final_norm if i == DEPTH - 1 else None)
    return x2d.reshape(BATCH, SEQ, D_MODEL)
```

```python
import functools

import jax
import jax.numpy as jnp
from jax import lax
from jax.experimental import pallas as pl
from jax.experimental.pallas import tpu as pltpu

D_MODEL = 1024
BATCH = 4
SEQ = 4096
DEPTH = 4
N_MIXERS = 3
HEAD_DIM = 64
N_Q_HEADS = D_MODEL // HEAD_DIM
N_KV_HEADS = 4
GQA_GROUP = N_Q_HEADS // N_KV_HEADS
QKV_WIDTH = (N_Q_HEADS + 2 * N_KV_HEADS) * HEAD_DIM
ROPE_THETA = 10000.0
SWA_WINDOW = 128
RET_HEADS = 4
RET_KEY_DIM = D_MODEL // RET_HEADS
RET_VAL_DIM = 2 * RET_KEY_DIM
RET_IN_WIDTH = 2 * D_MODEL + 2 * RET_HEADS * RET_VAL_DIM
RET_CHUNK = 128
MOBA_BLOCK = 256
MOBA_TOPK = 3
D_FF = ((8 * D_MODEL // 3 + 127) // 128) * 128
CONV_WIDTH = 3
EPS = 1e-6
NEG_INF = -1e30

LANES = 128
SUBLANES = 8
MXU_WIDTH = 256
VMEM_LIMIT_BYTES = 56 * 1024 * 1024

TOKENS = BATCH * SEQ
PROJ_TM = 512
ATT_TQ = 512
RET_TQ = 512
MOBA_ROW_CHUNK = 128
MOBA_GROUPS = 4
FFN_TM = 512
FFN_CHUNK = 256
FFN_HALO = SUBLANES

f32 = jnp.float32
bf16 = jnp.bfloat16
NT_DIMS = (((1,), (1,)), ((), ()))


def _params(*semantics):
    return pltpu.CompilerParams(dimension_semantics=semantics, vmem_limit_bytes=VMEM_LIMIT_BYTES)


def _rms_norm(x, gain):
    ms = jnp.mean(x * x, axis=-1, keepdims=True)
    return x * lax.rsqrt(ms + EPS) * gain


def _silu(x):
    return x * (0.5 + 0.5 * jnp.tanh(0.5 * x))


def _normalize_pair(pv_even, pv_odd, extra_even, extra_odd):
    low = lax.broadcasted_iota(jnp.int32, pv_even.shape, 1) < HEAD_DIM
    num = jnp.where(low, pv_even, pltpu.roll(pv_odd, HEAD_DIM, 1))
    den = jnp.where(low, pltpu.roll(pv_even, HEAD_DIM, 1) + extra_even, pv_odd + extra_odd)
    return num / den


def _attn_proj_kernel(x_ref, g_ref, w_ref, cos_ref, sin_ref, q_ref, k_ref, v_ref):
    t = pl.program_id(1)
    h = _rms_norm(x_ref[...], g_ref[...]).astype(bf16)
    cosf = cos_ref[...]
    sins = sin_ref[...]
    lane = lax.broadcasted_iota(jnp.int32, (PROJ_TM, LANES), 1)
    first_half = (lane % HEAD_DIM) < (HEAD_DIM // 2)
    low = lane < HEAD_DIM
    blk = (t * PROJ_TM + lax.broadcasted_iota(jnp.int32, (PROJ_TM, LANES), 0)) // MOBA_BLOCK
    k_tail = jnp.where(lane == HEAD_DIM + blk, 1.0, 0.0)
    v_tail = 1.0

    def split(r, tail):
        return (jnp.where(low, r, tail).astype(bf16),
                jnp.where(low, pltpu.roll(r, HEAD_DIM, 1), tail).astype(bf16))

    def rope(xc):
        partner = jnp.where(first_half, pltpu.roll(xc, LANES - HEAD_DIM // 2, 1), pltpu.roll(xc, HEAD_DIM // 2, 1))
        return xc * cosf + partner * sins

    n_q = N_Q_HEADS * HEAD_DIM // LANES
    n_k = N_KV_HEADS * HEAD_DIM // LANES
    groups_per_dot = MXU_WIDTH // LANES
    for d in range(QKV_WIDTH // MXU_WIDTH):
        proj = jnp.dot(h, w_ref[:, d * MXU_WIDTH:(d + 1) * MXU_WIDTH], preferred_element_type=f32)
        for e in range(groups_per_dot):
            c = d * groups_per_dot + e
            xc = proj[:, e * LANES:(e + 1) * LANES]
            if c < n_q:
                q_ref[2 * c], q_ref[2 * c + 1] = split(rope(xc) * (HEAD_DIM ** -0.5), 0.0)
            elif c < n_q + n_k:
                k_ref[2 * (c - n_q)], k_ref[2 * (c - n_q) + 1] = split(rope(xc), k_tail)
            else:
                v_ref[2 * (c - n_q - n_k)], v_ref[2 * (c - n_q - n_k) + 1] = split(xc, v_tail)


def _attn_proj(x, gain, w_bf16, cosf, sins):
    nt = SEQ // PROJ_TM
    return pl.pallas_call(
        _attn_proj_kernel,
        grid=(BATCH, nt),
        in_specs=[
            pl.BlockSpec((None, PROJ_TM, D_MODEL), lambda b, t: (b, t, 0)),
            pl.BlockSpec((1, D_MODEL), lambda b, t: (0, 0)),
            pl.BlockSpec((D_MODEL, QKV_WIDTH), lambda b, t: (0, 0)),
            pl.BlockSpec((None, PROJ_TM, LANES), lambda b, t: (b, t, 0)),
            pl.BlockSpec((None, PROJ_TM, LANES), lambda b, t: (b, t, 0)),
        ],
        out_specs=[
            pl.BlockSpec((None, N_Q_HEADS, PROJ_TM, LANES), lambda b, t: (b, 0, t, 0)),
            pl.BlockSpec((None, N_KV_HEADS, PROJ_TM, LANES), lambda b, t: (b, 0, t, 0)),
            pl.BlockSpec((None, N_KV_HEADS, PROJ_TM, LANES), lambda b, t: (b, 0, t, 0)),
        ],
        out_shape=[
            jax.ShapeDtypeStruct((BATCH, N_Q_HEADS, SEQ, LANES), bf16),
            jax.ShapeDtypeStruct((BATCH, N_KV_HEADS, SEQ, LANES), bf16),
            jax.ShapeDtypeStruct((BATCH, N_KV_HEADS, SEQ, LANES), bf16),
        ],
        compiler_params=_params("parallel", "parallel"),
        name="attn_proj",
    )(x, gain.reshape(1, D_MODEL), w_bf16, cosf, sins)


def _swa_kernel(sinks_ref, q_ref, k_ref, v_ref, o_ref, s_ref):
    g = pl.program_id(1)
    t = pl.program_id(2)
    W = SWA_WINDOW
    nsub = ATT_TQ // W
    qi = lax.broadcasted_iota(jnp.int32, (W, 2 * W), 0)
    col = lax.broadcasted_iota(jnp.int32, (W, 2 * W), 1)
    band = jnp.where((col > qi) & (col <= qi + W), 0.0, NEG_INF)
    first = jnp.where(col <= qi, 0.0, NEG_INF)

    def key_start(c):
        return pl.multiple_of(jnp.maximum(t * ATT_TQ + (c - 1) * W, 0), W)

    def scores(c):
        q4 = q_ref[:, c * W:(c + 1) * W, :].reshape(GQA_GROUP * W, LANES)
        kk = k_ref[pl.ds(key_start(c), 2 * W), :]
        s_ref[c] = lax.dot_general(q4, kk, NT_DIMS, preferred_element_type=f32)

    def head(c, j, vv, bias):
        sink = sinks_ref[g * GQA_GROUP + j]
        s = s_ref[c, j * W:(j + 1) * W, :] + bias
        s0, s1 = s[:, :LANES], s[:, LANES:]
        m = jnp.maximum(jnp.max(jnp.maximum(s0, s1), axis=1, keepdims=True), sink)
        p = jnp.concatenate([jnp.exp(s0 - m), jnp.exp(s1 - m)], axis=1).astype(bf16)
        return jnp.dot(p, vv, preferred_element_type=f32), jnp.exp(sink - m)

    scores(0)
    for c in range(nsub):
        if c + 1 < nsub:
            scores(c + 1)
        vv = v_ref[pl.ds(key_start(c), 2 * W), :]
        bias = jnp.where(t == 0, first, band) if c == 0 else band
        for jj in range(GQA_GROUP // 2):
            pv0, e0 = head(c, 2 * jj, vv, bias)
            pv1, e1 = head(c, 2 * jj + 1, vv, bias)
            o_ref[c * W:(c + 1) * W, jj * LANES:(jj + 1) * LANES] = _normalize_pair(pv0, pv1, e0, e1).astype(bf16)


def _swa(q, k, v, sinks):
    nt = SEQ // ATT_TQ
    gw = GQA_GROUP * HEAD_DIM
    return pl.pallas_call(
        _swa_kernel,
        grid=(BATCH, N_KV_HEADS, nt),
        in_specs=[
            pl.BlockSpec(memory_space=pltpu.SMEM),
            pl.BlockSpec((None, GQA_GROUP, ATT_TQ, LANES), lambda b, g, t: (b, g, t, 0)),
            pl.BlockSpec((None, None, SEQ, LANES), lambda b, g, t: (b, g, 0, 0)),
            pl.BlockSpec((None, None, SEQ, LANES), lambda b, g, t: (b, g, 0, 0)),
        ],
        out_specs=pl.BlockSpec((None, ATT_TQ, gw), lambda b, g, t: (b, t, g)),
        out_shape=jax.ShapeDtypeStruct((BATCH, SEQ, D_MODEL), bf16),
        scratch_shapes=[pltpu.VMEM((ATT_TQ // SWA_WINDOW, GQA_GROUP * SWA_WINDOW, 2 * SWA_WINDOW), f32)],
        compiler_params=_params("parallel", "parallel", "parallel"),
        name="swa_core",
    )(sinks, q, k, v)


def _moba_kernel(q_ref, k_ref, v_ref, o_ref, kmean_ref, qa_ref, sa_ref, sb_ref, m_ref, acc_ref):
    i = pl.program_id(2)
    BS = MOBA_BLOCK
    nb = SEQ // BS
    G = MOBA_GROUPS
    grows = GQA_GROUP * BS
    rows = G * grows

    @pl.when(i == 0)
    def _():
        lane = lax.broadcasted_iota(jnp.int32, (1, LANES), 1)
        for g in range(G):
            for n in range(nb):
                kb = k_ref[g, n * BS:(n + 1) * BS, :].astype(f32)
                kmean_ref[g, n:n + 1, :] = jnp.where(lane < HEAD_DIM, jnp.mean(kb, axis=0, keepdims=True), 0.0)

    q = q_ref[...].reshape(rows, LANES)
    gate_t = jnp.concatenate(
        [lax.dot_general(kmean_ref[g].astype(bf16), q[g * grows:(g + 1) * grows], NT_DIMS,
                         preferred_element_type=f32) for g in range(G)], axis=1)
    blk = lax.broadcasted_iota(jnp.int32, (nb, rows), 0)
    past = blk < i
    gm = jnp.where(past, gate_t, NEG_INF)
    sel = jnp.zeros((nb, rows), jnp.int32)
    for _ in range(MOBA_TOPK):
        mx = jnp.max(gm, axis=0, keepdims=True)
        cand = (gm == mx) & past & (sel == 0)
        idx = jnp.min(jnp.where(cand, blk, nb), axis=0, keepdims=True)
        pick = blk == idx
        sel = jnp.where(pick, 1, sel)
        gm = jnp.where(pick, NEG_INF, gm)
    bias_t = jnp.where((sel == 1) | jnp.logical_not(past), 0.0, NEG_INF)
    bias_pad_t = jnp.concatenate(
        [jnp.zeros((HEAD_DIM, rows), f32), bias_t, jnp.zeros((LANES - HEAD_DIM - nb, rows), f32)], axis=0)
    qa_ref[...] = (q.astype(f32) + bias_pad_t.T).astype(bf16)

    RC = MOBA_ROW_CHUNK
    qi = lax.broadcasted_iota(jnp.int32, (RC, LANES), 0)
    kj = lax.broadcasted_iota(jnp.int32, (RC, LANES), 1)

    def block_start(n):
        return pl.multiple_of(jnp.minimum(n, nb - 1) * BS, BS)

    def scores(n, s_ref):
        for g in range(G):
            gs = slice(g * grows, (g + 1) * grows)
            k_n = k_ref[g, pl.ds(block_start(n), BS), :]
            s_ref[gs, :] = lax.dot_general(qa_ref[gs, :], k_n, NT_DIMS, preferred_element_type=f32)

    def softmax_pv(n, s_ref, own):
        for g in range(G):
            v_n = v_ref[g, pl.ds(block_start(n), BS), :]
            for c in range(grows // RC):
                rs = slice(g * grows + c * RC, g * grows + (c + 1) * RC)
                s = s_ref[rs, :]
                s0, s1 = s[:, :LANES], s[:, LANES:]
                if own:
                    q0 = (c * RC) % BS
                    s0 = jnp.where(kj <= qi + q0, s0, NEG_INF)
                    s1 = jnp.where(kj + LANES <= qi + q0, s1, NEG_INF)
                row_max = jnp.max(jnp.maximum(s0, s1), axis=1, keepdims=True)
                if own:
                    m_new = jnp.broadcast_to(row_max, (RC, LANES))
                else:
                    m_old = m_ref[rs, :]
                    m_new = jnp.maximum(m_old, row_max)
                p = jnp.concatenate([jnp.exp(s0 - m_new), jnp.exp(s1 - m_new)], axis=1).astype(bf16)
                pv = jnp.dot(p, v_n, preferred_element_type=f32)
                if own:
                    acc_ref[rs, :] = pv
                else:
                    acc_ref[rs, :] = jnp.exp(m_old - m_new) * acc_ref[rs, :] + pv
                m_ref[rs, :] = m_new

    scores(i, sa_ref)
    scores(0, sb_ref)
    softmax_pv(i, sa_ref, True)

    def body(n, carry):
        @pl.when(n % 2 == 0)
        def _():
            scores(n + 1, sa_ref)
            softmax_pv(n, sb_ref, False)

        @pl.when(n % 2 == 1)
        def _():
            scores(n + 1, sb_ref)
            softmax_pv(n, sa_ref, False)

        return carry

    lax.fori_loop(0, i, body, 0)
    for jj in range(G * GQA_GROUP // 2):
        a0 = acc_ref[2 * jj * BS:(2 * jj + 1) * BS, :]
        a1 = acc_ref[(2 * jj + 1) * BS:(2 * jj + 2) * BS, :]
        o_ref[:, jj * LANES:(jj + 1) * LANES] = _normalize_pair(a0, a1, 0.0, 0.0).astype(bf16)


def _moba(q, k, v):
    nb = SEQ // MOBA_BLOCK
    G = MOBA_GROUPS
    gw = G * GQA_GROUP * HEAD_DIM
    rows = G * GQA_GROUP * MOBA_BLOCK
    return pl.pallas_call(
        _moba_kernel,
        grid=(BATCH, N_KV_HEADS // G, nb),
        in_specs=[
            pl.BlockSpec((None, G * GQA_GROUP, MOBA_BLOCK, LANES), lambda b, g, i: (b, g, i, 0)),
            pl.BlockSpec((None, G, SEQ, LANES), lambda b, g, i: (b, g, 0, 0)),
            pl.BlockSpec((None, G, SEQ, LANES), lambda b, g, i: (b, g, 0, 0)),
        ],
        out_specs=pl.BlockSpec((None, MOBA_BLOCK, gw), lambda b, g, i: (b, i, g)),
        out_shape=jax.ShapeDtypeStruct((BATCH, SEQ, D_MODEL), bf16),
        scratch_shapes=[
            pltpu.VMEM((G, nb, LANES), f32),
            pltpu.VMEM((rows, LANES), bf16),
            pltpu.VMEM((rows, MOBA_BLOCK), f32),
            pltpu.VMEM((rows, MOBA_BLOCK), f32),
            pltpu.VMEM((rows, LANES), f32),
            pltpu.VMEM((rows, LANES), f32),
        ],
        compiler_params=_params("parallel", "parallel", "arbitrary"),
        name="moba_core",
    )(q, k, v)


def _ret_proj_kernel(x_ref, g_ref, w_ref, cos_ref, sin_ref, o_ref):
    h = _rms_norm(x_ref[...], g_ref[...]).astype(bf16)
    cos = cos_ref[...]
    sin = sin_ref[...]
    half = RET_KEY_DIM // 2
    n_rope = 2 * RET_HEADS
    for c in range(RET_IN_WIDTH // RET_KEY_DIM):
        c0 = c * RET_KEY_DIM
        proj = jnp.dot(h, w_ref[:, c0:c0 + RET_KEY_DIM], preferred_element_type=f32)
        if c < n_rope:
            x1, x2 = proj[:, :half], proj[:, half:]
            r1, r2 = x1 * cos - x2 * sin, x2 * cos + x1 * sin
            if c >= RET_HEADS:
                r1, r2 = r1 * (RET_KEY_DIM ** -0.5), r2 * (RET_KEY_DIM ** -0.5)
            o_ref[:, c0:c0 + half] = r1.astype(bf16)
            o_ref[:, c0 + half:c0 + RET_KEY_DIM] = r2.astype(bf16)
        else:
            o_ref[:, c0:c0 + RET_KEY_DIM] = proj.astype(bf16)


def _ret_proj(x2d, gain, w_bf16, cos, sin):
    nt = TOKENS // PROJ_TM
    return pl.pallas_call(
        _ret_proj_kernel,
        grid=(nt,),
        in_specs=[
            pl.BlockSpec((PROJ_TM, D_MODEL), lambda t: (t, 0)),
            pl.BlockSpec((1, D_MODEL), lambda t: (0, 0)),
            pl.BlockSpec((D_MODEL, RET_IN_WIDTH), lambda t: (0, 0), pipeline_mode=pl.Buffered(1)),
            pl.BlockSpec((PROJ_TM, LANES), lambda t: (t, 0)),
            pl.BlockSpec((PROJ_TM, LANES), lambda t: (t, 0)),
        ],
        out_specs=pl.BlockSpec((PROJ_TM, RET_IN_WIDTH), lambda t: (t, 0)),
        out_shape=jax.ShapeDtypeStruct((TOKENS, RET_IN_WIDTH), bf16),
        compiler_params=_params("parallel"),
        name="ret_proj",
    )(x2d, gain.reshape(1, D_MODEL), w_bf16, cos, sin)


def _ret_core_kernel(cd_ref, q_ref, k_ref, v_ref, g_ref, din_ref, qdec_ref, kdec_ref, gn_ref, y_ref, r_ref):
    h = pl.program_id(1)
    t = pl.program_id(2)
    C = RET_CHUNK

    @pl.when(t == 0)
    def _():
        r_ref[...] = jnp.zeros_like(r_ref)

    decay_in = din_ref[...]
    qdec = qdec_ref[...]
    kdec = kdec_ref[...]
    cd = cd_ref[h]
    gn = gn_ref[...]
    for c in range(RET_TQ // C):
        sl = slice(c * C, (c + 1) * C)
        qn = q_ref[sl, :]
        kn = k_ref[sl, :]
        vn = v_ref[sl, :]
        sc = lax.dot_general(qn, kn, NT_DIMS, preferred_element_type=f32) * decay_in
        inner = jnp.dot(sc.astype(bf16), vn, preferred_element_type=f32)
        r_old = r_ref[...]
        qd = (qn.astype(f32) * qdec).astype(bf16)
        cross = jnp.dot(qd, r_old.astype(bf16), preferred_element_type=f32)
        kdt = (kn.astype(f32) * kdec).T.astype(bf16)
        r_ref[...] = cd * r_old + jnp.dot(kdt, vn, preferred_element_type=f32)
        o = inner + cross
        mu = jnp.mean(o, axis=-1, keepdims=True)
        d = o - mu
        var = jnp.mean(d * d, axis=-1, keepdims=True)
        on = d * lax.rsqrt(var + EPS) * gn
        y_ref[sl, :] = (_silu(g_ref[sl, :].astype(f32)) * on).astype(bf16)


def _ret_core(proj3d, gn_g, decay_in, q_decay, k_decay, chunk_decay):
    nt = SEQ // RET_TQ
    H, dk, dv, C = RET_HEADS, RET_KEY_DIM, RET_VAL_DIM, RET_CHUNK
    v_blk0 = 2 * D_MODEL // dv
    g_blk0 = (2 * D_MODEL + H * dv) // dv
    return pl.pallas_call(
        _ret_core_kernel,
        grid=(BATCH, H, nt),
        in_specs=[
            pl.BlockSpec(memory_space=pltpu.SMEM),
            pl.BlockSpec((None, RET_TQ, dk), lambda b, h, t: (b, t, h)),
            pl.BlockSpec((None, RET_TQ, dk), lambda b, h, t: (b, t, H + h)),
            pl.BlockSpec((None, RET_TQ, dv), lambda b, h, t: (b, t, v_blk0 + h)),
            pl.BlockSpec((None, RET_TQ, dv), lambda b, h, t: (b, t, g_blk0 + h)),
            pl.BlockSpec((None, C, C), lambda b, h, t: (h, 0, 0)),
            pl.BlockSpec((None, C, 1), lambda b, h, t: (h, 0, 0)),
            pl.BlockSpec((None, C, 1), lambda b, h, t: (h, 0, 0)),
            pl.BlockSpec((1, dv), lambda b, h, t: (0, h)),
        ],
        out_specs=pl.BlockSpec((None, RET_TQ, dv), lambda b, h, t: (b, t, h)),
        out_shape=jax.ShapeDtypeStruct((BATCH, SEQ, H * dv), bf16),
        scratch_shapes=[pltpu.VMEM((dk, dv), f32)],
        compiler_params=_params("parallel", "parallel", "arbitrary"),
        name="ret_core",
    )(chunk_decay, proj3d, proj3d, proj3d, proj3d, decay_in, q_decay, k_decay, gn_g.reshape(1, H * dv))


def _ffn_kernel(y_ref, wo_ref, x_ref, g_ref, wa_ref, wb_ref, cw_ref, cb_ref, wd_ref, fg_ref, o_ref,
                aprev_ref, h_ref, hid_ref, *, final_norm):
    t = pl.program_id(0)

    @pl.when(t == 0)
    def _():
        aprev_ref[...] = jnp.zeros_like(aprev_ref)

    x1 = x_ref[...] + jnp.dot(y_ref[...], wo_ref[...], preferred_element_type=f32)
    o_ref[...] = x1
    h_ref[...] = _rms_norm(x1, g_ref[...]).astype(bf16)
    seq_start = (t * FFN_TM) % SEQ == 0
    for c0 in range(0, D_FF, FFN_CHUNK):
        cs = slice(c0, min(c0 + FFN_CHUNK, D_FF))
        a = jnp.dot(h_ref[...], wa_ref[:, cs], preferred_element_type=f32)
        b = jnp.dot(h_ref[...], wb_ref[:, cs], preferred_element_type=f32)
        prev = jnp.where(seq_start, 0.0, aprev_ref[:, cs])
        aprev_ref[:, cs] = a[FFN_TM - FFN_HALO:, :]
        a_full = jnp.concatenate([prev, a], axis=0)
        a1 = pltpu.roll(a_full, 1, 0)[FFN_HALO:]
        a2 = pltpu.roll(a_full, 2, 0)[FFN_HALO:]
        cw = cw_ref[:, cs]
        conv = cw[0:1] * a2 + cw[1:2] * a1 + cw[2:3] * a + cb_ref[:, cs]
        hid_ref[:, cs] = (_silu(conv) * b).astype(bf16)
    out = o_ref[...] + jnp.dot(hid_ref[...], wd_ref[...], preferred_element_type=f32)
    o_ref[...] = _rms_norm(out, fg_ref[...]) if final_norm else out


def _ffn(y2d, w_out, x2d, gain, wa, wb, conv_w, conv_b, wd, final_gain):
    kdim = y2d.shape[1]
    nt = TOKENS // FFN_TM
    resident = dict(pipeline_mode=pl.Buffered(1))
    final_norm = final_gain is not None
    fg = final_gain if final_norm else gain
    return pl.pallas_call(
        functools.partial(_ffn_kernel, final_norm=final_norm),
        grid=(nt,),
        in_specs=[
            pl.BlockSpec((FFN_TM, kdim), lambda t: (t, 0)),
            pl.BlockSpec((kdim, D_MODEL), lambda t: (0, 0), **resident),
            pl.BlockSpec((FFN_TM, D_MODEL), lambda t: (t, 0)),
            pl.BlockSpec((1, D_MODEL), lambda t: (0, 0)),
            pl.BlockSpec((D_MODEL, D_FF), lambda t: (0, 0), **resident),
            pl.BlockSpec((D_MODEL, D_FF), lambda t: (0, 0), **resident),
            pl.BlockSpec((CONV_WIDTH, D_FF), lambda t: (0, 0)),
            pl.BlockSpec((1, D_FF), lambda t: (0, 0)),
            pl.BlockSpec((D_FF, D_MODEL), lambda t: (0, 0), **resident),
            pl.BlockSpec((1, D_MODEL), lambda t: (0, 0)),
        ],
        out_specs=pl.BlockSpec((FFN_TM, D_MODEL), lambda t: (t, 0)),
        out_shape=jax.ShapeDtypeStruct((TOKENS, D_MODEL), f32),
        scratch_shapes=[
            pltpu.VMEM((FFN_HALO, D_FF), f32),
            pltpu.VMEM((FFN_TM, D_MODEL), bf16),
            pltpu.VMEM((FFN_TM, D_FF), bf16),
        ],
        compiler_params=_params("arbitrary"),
        name="out_proj_conv_ffn",
    )(y2d, w_out, x2d, gain.reshape(1, D_MODEL), wa, wb, conv_w, conv_b.reshape(1, D_FF), wd, fg.reshape(1, D_MODEL))


def _rope_angles(positions, dim):
    inv = 1.0 / (ROPE_THETA ** (jnp.arange(0, dim, 2, dtype=f32) / dim))
    return positions.astype(f32)[..., None] * inv


def _head_rope_tables(positions):
    ang = _rope_angles(positions, HEAD_DIM)
    cos, sin = jnp.cos(ang), jnp.sin(ang)
    reps = LANES // HEAD_DIM
    cosf = jnp.tile(jnp.concatenate([cos, cos], axis=-1), (1, 1, reps))
    sins = jnp.tile(jnp.concatenate([-sin, sin], axis=-1), (1, 1, reps))
    return cosf, sins


def _retention_decays():
    H, C = RET_HEADS, RET_CHUNK
    log_gamma = jnp.log(1.0 - 2.0 ** (-5.0 - jnp.arange(H, dtype=f32)))
    idx = jnp.arange(C, dtype=f32)
    diff = idx[:, None] - idx[None, :]
    decay_in = jnp.where(diff[None] >= 0, jnp.exp(jnp.maximum(diff, 0.0)[None] * log_gamma[:, None, None]), 0.0)
    q_decay = jnp.exp((idx + 1.0)[None, :] * log_gamma[:, None])
    k_decay = jnp.exp((C - 1.0 - idx)[None, :] * log_gamma[:, None])
    chunk_decay = jnp.exp(C * log_gamma)
    return decay_in, q_decay[..., None], k_decay[..., None], chunk_decay


def kernel(x, positions, l0_attn_norm, l0_w_in, l0_w_out, l0_sinks, l0_ffn_norm, l0_w_a, l0_w_b, l0_conv_w, l0_conv_b, l0_w_down, l1_attn_norm, l1_w_in, l1_w_out, l1_gn_g, l1_ffn_norm, l1_w_a, l1_w_b, l1_conv_w, l1_conv_b, l1_w_down, l2_attn_norm, l2_w_in, l2_w_out, l2_ffn_norm, l2_w_a, l2_w_b, l2_conv_w, l2_conv_b, l2_w_down, l3_attn_norm, l3_w_in, l3_w_out, l3_sinks, l3_ffn_norm, l3_w_a, l3_w_b, l3_conv_w, l3_conv_b, l3_w_down, final_norm):
    mixers = [
        (l0_attn_norm, l0_w_in, l0_w_out, l0_sinks),
        (l1_attn_norm, l1_w_in, l1_w_out, l1_gn_g),
        (l2_attn_norm, l2_w_in, l2_w_out, None),
        (l3_attn_norm, l3_w_in, l3_w_out, l3_sinks),
    ]
    ffns = [
        (l0_ffn_norm, l0_w_a, l0_w_b, l0_conv_w, l0_conv_b, l0_w_down),
        (l1_ffn_norm, l1_w_a, l1_w_b, l1_conv_w, l1_conv_b, l1_w_down),
        (l2_ffn_norm, l2_w_a, l2_w_b, l2_conv_w, l2_conv_b, l2_w_down),
        (l3_ffn_norm, l3_w_a, l3_w_b, l3_conv_w, l3_conv_b, l3_w_down),
    ]
    cosf, sins = _head_rope_tables(positions)
    ang_r = _rope_angles(positions, RET_KEY_DIM).reshape(TOKENS, RET_KEY_DIM // 2)
    cos_r, sin_r = jnp.cos(ang_r), jnp.sin(ang_r)
    decay_in, q_decay, k_decay, chunk_decay = _retention_decays()

    x2d = x.reshape(TOKENS, D_MODEL)
    for i in range(DEPTH):
        norm_g, w_in, w_out, extra = mixers[i]
        m = i % N_MIXERS
        w_in_b = w_in.astype(bf16)
        if m == 1:
            proj = _ret_proj(x2d, norm_g, w_in_b, cos_r, sin_r)
            y = _ret_core(proj.reshape(BATCH, SEQ, RET_IN_WIDTH), extra, decay_in, q_decay, k_decay, chunk_decay)
            y2d = y.reshape(TOKENS, RET_HEADS * RET_VAL_DIM)
        else:
            q, k, v = _attn_proj(x2d.reshape(BATCH, SEQ, D_MODEL), norm_g, w_in_b, cosf, sins)
            y = _swa(q, k, v, extra) if m == 0 else _moba(q, k, v)
            y2d = y.reshape(TOKENS, D_MODEL)
        fg, wa, wb, cw, cb, wd = ffns[i]
        x2d = _ffn(y2d, w_out.astype(bf16), x2d, fg, wa.astype(bf16), wb.astype(bf16), cw, cb, wd.astype(bf16),
                   final_norm if i == DEPTH - 1 else None)
    return x2d.reshape(BATCH, SEQ, D_MODEL)
```

```python
import functools

import jax
import jax.numpy as jnp
from jax import lax
from jax.experimental import pallas as pl
from jax.experimental.pallas import tpu as pltpu

D_MODEL = 1024
BATCH = 4
SEQ = 4096
DEPTH = 4
N_MIXERS = 3
HEAD_DIM = 64
N_Q_HEADS = D_MODEL // HEAD_DIM
N_KV_HEADS = 4
GQA_GROUP = N_Q_HEADS // N_KV_HEADS
QKV_WIDTH = (N_Q_HEADS + 2 * N_KV_HEADS) * HEAD_DIM
ROPE_THETA = 10000.0
SWA_WINDOW = 128
RET_HEADS = 4
RET_KEY_DIM = D_MODEL // RET_HEADS
RET_VAL_DIM = 2 * RET_KEY_DIM
RET_IN_WIDTH = 2 * D_MODEL + 2 * RET_HEADS * RET_VAL_DIM
RET_CHUNK = 128
MOBA_BLOCK = 256
MOBA_TOPK = 3
D_FF = ((8 * D_MODEL // 3 + 127) // 128) * 128
CONV_WIDTH = 3
EPS = 1e-6
NEG_INF = -1e30

LANES = 128
SUBLANES = 8
MXU_WIDTH = 256
VMEM_LIMIT_BYTES = 56 * 1024 * 1024

TOKENS = BATCH * SEQ
PROJ_TM = 512
ATT_TQ = 512
RET_TQ = 512
MOBA_ROW_CHUNK = 128
MOBA_GROUPS = 4
FFN_TM = 512
FFN_CHUNK = 256
FFN_HALO = SUBLANES

f32 = jnp.float32
bf16 = jnp.bfloat16
NT_DIMS = (((1,), (1,)), ((), ()))


def _params(*semantics):
    return pltpu.CompilerParams(dimension_semantics=semantics, vmem_limit_bytes=VMEM_LIMIT_BYTES)


def _rms_norm(x, gain):
    ms = jnp.mean(x * x, axis=-1, keepdims=True)
    return x * lax.rsqrt(ms + EPS) * gain


def _silu(x):
    half = 0.5 * x
    return half + half * jnp.tanh(half)


def _normalize_pair(pv_even, pv_odd, extra_even, extra_odd):
    low = lax.broadcasted_iota(jnp.int32, pv_even.shape, 1) < HEAD_DIM
    num = jnp.where(low, pv_even, pltpu.roll(pv_odd, HEAD_DIM, 1))
    den = jnp.where(low, pltpu.roll(pv_even, HEAD_DIM, 1) + extra_even, pv_odd + extra_odd)
    return num / den


def _attn_proj_kernel(x_ref, g_ref, w_ref, cos_ref, sin_ref, q_ref, k_ref, v_ref):
    t = pl.program_id(1)
    h = _rms_norm(x_ref[...], g_ref[...]).astype(bf16)
    cosf = cos_ref[...]
    sins = sin_ref[...]
    lane = lax.broadcasted_iota(jnp.int32, (PROJ_TM, LANES), 1)
    first_half = (lane % HEAD_DIM) < (HEAD_DIM // 2)
    low = lane < HEAD_DIM
    blk = (t * PROJ_TM + lax.broadcasted_iota(jnp.int32, (PROJ_TM, LANES), 0)) // MOBA_BLOCK
    k_tail = jnp.where(lane == HEAD_DIM + blk, 1.0, 0.0)
    v_tail = 1.0

    def split(r, tail):
        return (jnp.where(low, r, tail).astype(bf16),
                jnp.where(low, pltpu.roll(r, HEAD_DIM, 1), tail).astype(bf16))

    def rope(xc):
        partner = jnp.where(first_half, pltpu.roll(xc, LANES - HEAD_DIM // 2, 1), pltpu.roll(xc, HEAD_DIM // 2, 1))
        return xc * cosf + partner * sins

    n_q = N_Q_HEADS * HEAD_DIM // LANES
    n_k = N_KV_HEADS * HEAD_DIM // LANES
    groups_per_dot = MXU_WIDTH // LANES
    for d in range(QKV_WIDTH // MXU_WIDTH):
        proj = jnp.dot(h, w_ref[:, d * MXU_WIDTH:(d + 1) * MXU_WIDTH], preferred_element_type=f32)
        for e in range(groups_per_dot):
            c = d * groups_per_dot + e
            xc = proj[:, e * LANES:(e + 1) * LANES]
            if c < n_q:
                q_ref[2 * c], q_ref[2 * c + 1] = split(rope(xc) * (HEAD_DIM ** -0.5), 0.0)
            elif c < n_q + n_k:
                k_ref[2 * (c - n_q)], k_ref[2 * (c - n_q) + 1] = split(rope(xc), k_tail)
            else:
                v_ref[2 * (c - n_q - n_k)], v_ref[2 * (c - n_q - n_k) + 1] = split(xc, v_tail)


def _attn_proj(x, gain, w_bf16, cosf, sins):
    nt = SEQ // PROJ_TM
    return pl.pallas_call(
        _attn_proj_kernel,
        grid=(BATCH, nt),
        in_specs=[
            pl.BlockSpec((None, PROJ_TM, D_MODEL), lambda b, t: (b, t, 0)),
            pl.BlockSpec((1, D_MODEL), lambda b, t: (0, 0)),
            pl.BlockSpec((D_MODEL, QKV_WIDTH), lambda b, t: (0, 0)),
            pl.BlockSpec((None, PROJ_TM, LANES), lambda b, t: (b, t, 0)),
            pl.BlockSpec((None, PROJ_TM, LANES), lambda b, t: (b, t, 0)),
        ],
        out_specs=[
            pl.BlockSpec((None, N_Q_HEADS, PROJ_TM, LANES), lambda b, t: (b, 0, t, 0)),
            pl.BlockSpec((None, N_KV_HEADS, PROJ_TM, LANES), lambda b, t: (b, 0, t, 0)),
            pl.BlockSpec((None, N_KV_HEADS, PROJ_TM, LANES), lambda b, t: (b, 0, t, 0)),
        ],
        out_shape=[
            jax.ShapeDtypeStruct((BATCH, N_Q_HEADS, SEQ, LANES), bf16),
            jax.ShapeDtypeStruct((BATCH, N_KV_HEADS, SEQ, LANES), bf16),
            jax.ShapeDtypeStruct((BATCH, N_KV_HEADS, SEQ, LANES), bf16),
        ],
        compiler_params=_params("parallel", "parallel"),
        name="attn_proj",
    )(x, gain.reshape(1, D_MODEL), w_bf16, cosf, sins)


def _swa_kernel(sinks_ref, q_ref, k_ref, v_ref, o_ref, s_ref):
    g = pl.program_id(1)
    t = pl.program_id(2)
    W = SWA_WINDOW
    nsub = ATT_TQ // W
    qi = lax.broadcasted_iota(jnp.int32, (W, 2 * W), 0)
    col = lax.broadcasted_iota(jnp.int32, (W, 2 * W), 1)
    band = jnp.where((col > qi) & (col <= qi + W), 0.0, NEG_INF)
    first = jnp.where(col <= qi, 0.0, NEG_INF)

    def key_start(c):
        return pl.multiple_of(jnp.maximum(t * ATT_TQ + (c - 1) * W, 0), W)

    def scores(c):
        q4 = q_ref[:, c * W:(c + 1) * W, :].reshape(GQA_GROUP * W, LANES)
        kk = k_ref[pl.ds(key_start(c), 2 * W), :]
        s_ref[c] = lax.dot_general(q4, kk, NT_DIMS, preferred_element_type=f32)

    def head(c, j, vv, bias):
        sink = sinks_ref[g * GQA_GROUP + j]
        s = s_ref[c, j * W:(j + 1) * W, :] + bias
        s0, s1 = s[:, :LANES], s[:, LANES:]
        m = jnp.maximum(jnp.max(jnp.maximum(s0, s1), axis=1, keepdims=True), sink)
        p = jnp.concatenate([jnp.exp(s0 - m), jnp.exp(s1 - m)], axis=1).astype(bf16)
        return jnp.dot(p, vv, preferred_element_type=f32), jnp.exp(sink - m)

    scores(0)
    for c in range(nsub):
        if c + 1 < nsub:
            scores(c + 1)
        vv = v_ref[pl.ds(key_start(c), 2 * W), :]
        bias = jnp.where(t == 0, first, band) if c == 0 else band
        for jj in range(GQA_GROUP // 2):
            pv0, e0 = head(c, 2 * jj, vv, bias)
            pv1, e1 = head(c, 2 * jj + 1, vv, bias)
            o_ref[c * W:(c + 1) * W, jj * LANES:(jj + 1) * LANES] = _normalize_pair(pv0, pv1, e0, e1).astype(bf16)


def _swa(q, k, v, sinks):
    nt = SEQ // ATT_TQ
    gw = GQA_GROUP * HEAD_DIM
    return pl.pallas_call(
        _swa_kernel,
        grid=(BATCH, N_KV_HEADS, nt),
        in_specs=[
            pl.BlockSpec(memory_space=pltpu.SMEM),
            pl.BlockSpec((None, GQA_GROUP, ATT_TQ, LANES), lambda b, g, t: (b, g, t, 0)),
            pl.BlockSpec((None, None, SEQ, LANES), lambda b, g, t: (b, g, 0, 0)),
            pl.BlockSpec((None, None, SEQ, LANES), lambda b, g, t: (b, g, 0, 0)),
        ],
        out_specs=pl.BlockSpec((None, ATT_TQ, gw), lambda b, g, t: (b, t, g)),
        out_shape=jax.ShapeDtypeStruct((BATCH, SEQ, D_MODEL), bf16),
        scratch_shapes=[pltpu.VMEM((ATT_TQ // SWA_WINDOW, GQA_GROUP * SWA_WINDOW, 2 * SWA_WINDOW), f32)],
        compiler_params=_params("parallel", "parallel", "parallel"),
        name="swa_core",
    )(sinks, q, k, v)


def _moba_kernel(q_ref, k_ref, v_ref, o_ref, kmean_ref, qa_ref, sa_ref, sb_ref, m_ref, acc_ref):
    i = pl.program_id(2)
    BS = MOBA_BLOCK
    nb = SEQ // BS
    G = MOBA_GROUPS
    grows = GQA_GROUP * BS
    rows = G * grows

    @pl.when(i == 0)
    def _():
        lane = lax.broadcasted_iota(jnp.int32, (1, LANES), 1)
        for g in range(G):
            for n in range(nb):
                kb = k_ref[g, n * BS:(n + 1) * BS, :].astype(f32)
                kmean_ref[g, n:n + 1, :] = jnp.where(lane < HEAD_DIM, jnp.mean(kb, axis=0, keepdims=True), 0.0)

    q = q_ref[...].reshape(rows, LANES)
    gate_t = jnp.concatenate(
        [lax.dot_general(kmean_ref[g].astype(bf16), q[g * grows:(g + 1) * grows], NT_DIMS,
                         preferred_element_type=f32) for g in range(G)], axis=1)
    blk = lax.broadcasted_iota(jnp.int32, (nb, rows), 0)
    past = blk < i
    gm = jnp.where(past, gate_t, NEG_INF)
    sel = jnp.zeros((nb, rows), jnp.int32)
    for _ in range(MOBA_TOPK):
        mx = jnp.max(gm, axis=0, keepdims=True)
        cand = (gm == mx) & past & (sel == 0)
        idx = jnp.min(jnp.where(cand, blk, nb), axis=0, keepdims=True)
        pick = blk == idx
        sel = jnp.where(pick, 1, sel)
        gm = jnp.where(pick, NEG_INF, gm)
    bias_t = jnp.where((sel == 1) | jnp.logical_not(past), 0.0, NEG_INF)
    bias_pad_t = jnp.concatenate(
        [jnp.zeros((HEAD_DIM, rows), f32), bias_t, jnp.zeros((LANES - HEAD_DIM - nb, rows), f32)], axis=0)
    qa_ref[...] = (q.astype(f32) + bias_pad_t.T).astype(bf16)

    RC = MOBA_ROW_CHUNK
    qi = lax.broadcasted_iota(jnp.int32, (RC, LANES), 0)
    kj = lax.broadcasted_iota(jnp.int32, (RC, LANES), 1)

    def block_start(n):
        return pl.multiple_of(jnp.minimum(n, nb - 1) * BS, BS)

    def plain_rows(g):
        return q_ref[g * GQA_GROUP:(g + 1) * GQA_GROUP].reshape(grows, LANES)

    def biased_rows(g):
        return qa_ref[g * grows:(g + 1) * grows, :]

    def scores(n, s_ref, lhs_rows=biased_rows):
        for g in range(G):
            k_n = k_ref[g, pl.ds(block_start(n), BS), :]
            s_ref[g * grows:(g + 1) * grows, :] = lax.dot_general(lhs_rows(g), k_n, NT_DIMS,
                                                                  preferred_element_type=f32)

    def softmax_pv(n, s_ref, own):
        for g in range(G):
            v_n = v_ref[g, pl.ds(block_start(n), BS), :]
            for c in range(grows // RC):
                rs = slice(g * grows + c * RC, g * grows + (c + 1) * RC)
                s = s_ref[rs, :]
                s0, s1 = s[:, :LANES], s[:, LANES:]
                if own:
                    q0 = (c * RC) % BS
                    s0 = jnp.where(kj <= qi + q0, s0, NEG_INF)
                    s1 = jnp.where(kj + LANES <= qi + q0, s1, NEG_INF)
                row_max = jnp.max(jnp.maximum(s0, s1), axis=1, keepdims=True)
                if own:
                    m_new = jnp.broadcast_to(row_max, (RC, LANES))
                else:
                    m_old = m_ref[rs, :]
                    m_new = jnp.maximum(m_old, row_max)
                p = jnp.concatenate([jnp.exp(s0 - m_new), jnp.exp(s1 - m_new)], axis=1).astype(bf16)
                pv = jnp.dot(p, v_n, preferred_element_type=f32)
                if own:
                    acc_ref[rs, :] = pv
                else:
                    acc_ref[rs, :] = jnp.exp(m_old - m_new) * acc_ref[rs, :] + pv
                m_ref[rs, :] = m_new

    scores(i, sa_ref, plain_rows)
    scores(0, sb_ref)
    softmax_pv(i, sa_ref, True)

    def body(n, carry):
        @pl.when(n % 2 == 0)
        def _():
            scores(n + 1, sa_ref)
            softmax_pv(n, sb_ref, False)

        @pl.when(n % 2 == 1)
        def _():
            scores(n + 1, sb_ref)
            softmax_pv(n, sa_ref, False)

        return carry

    lax.fori_loop(0, i, body, 0)
    for jj in range(G * GQA_GROUP // 2):
        a0 = acc_ref[2 * jj * BS:(2 * jj + 1) * BS, :]
        a1 = acc_ref[(2 * jj + 1) * BS:(2 * jj + 2) * BS, :]
        o_ref[:, jj * LANES:(jj + 1) * LANES] = _normalize_pair(a0, a1, 0.0, 0.0).astype(bf16)


def _moba(q, k, v):
    nb = SEQ // MOBA_BLOCK
    G = MOBA_GROUPS
    gw = G * GQA_GROUP * HEAD_DIM
    rows = G * GQA_GROUP * MOBA_BLOCK
    return pl.pallas_call(
        _moba_kernel,
        grid=(BATCH, N_KV_HEADS // G, nb),
        in_specs=[
            pl.BlockSpec((None, G * GQA_GROUP, MOBA_BLOCK, LANES), lambda b, g, i: (b, g, i, 0)),
            pl.BlockSpec((None, G, SEQ, LANES), lambda b, g, i: (b, g, 0, 0)),
            pl.BlockSpec((None, G, SEQ, LANES), lambda b, g, i: (b, g, 0, 0)),
        ],
        out_specs=pl.BlockSpec((None, MOBA_BLOCK, gw), lambda b, g, i: (b, i, g)),
        out_shape=jax.ShapeDtypeStruct((BATCH, SEQ, D_MODEL), bf16),
        scratch_shapes=[
            pltpu.VMEM((G, nb, LANES), f32),
            pltpu.VMEM((rows, LANES), bf16),
            pltpu.VMEM((rows, MOBA_BLOCK), f32),
            pltpu.VMEM((rows, MOBA_BLOCK), f32),
            pltpu.VMEM((rows, LANES), f32),
            pltpu.VMEM((rows, LANES), f32),
        ],
        compiler_params=_params("parallel", "parallel", "arbitrary"),
        name="moba_core",
    )(q, k, v)


def _ret_proj_kernel(x_ref, g_ref, w_ref, cos_ref, sin_ref, o_ref):
    h = _rms_norm(x_ref[...], g_ref[...]).astype(bf16)
    cos = cos_ref[...]
    sin = sin_ref[...]
    half = RET_KEY_DIM // 2
    n_rope = 2 * RET_HEADS
    for c in range(RET_IN_WIDTH // RET_KEY_DIM):
        c0 = c * RET_KEY_DIM
        proj = jnp.dot(h, w_ref[:, c0:c0 + RET_KEY_DIM], preferred_element_type=f32)
        if c < n_rope:
            x1, x2 = proj[:, :half], proj[:, half:]
            r1, r2 = x1 * cos - x2 * sin, x2 * cos + x1 * sin
            if c >= RET_HEADS:
                r1, r2 = r1 * (RET_KEY_DIM ** -0.5), r2 * (RET_KEY_DIM ** -0.5)
            o_ref[:, c0:c0 + half] = r1.astype(bf16)
            o_ref[:, c0 + half:c0 + RET_KEY_DIM] = r2.astype(bf16)
        else:
            o_ref[:, c0:c0 + RET_KEY_DIM] = proj.astype(bf16)


def _ret_proj(x2d, gain, w_bf16, cos, sin):
    nt = TOKENS // PROJ_TM
    return pl.pallas_call(
        _ret_proj_kernel,
        grid=(nt,),
        in_specs=[
            pl.BlockSpec((PROJ_TM, D_MODEL), lambda t: (t, 0)),
            pl.BlockSpec((1, D_MODEL), lambda t: (0, 0)),
            pl.BlockSpec((D_MODEL, RET_IN_WIDTH), lambda t: (0, 0), pipeline_mode=pl.Buffered(1)),
            pl.BlockSpec((PROJ_TM, LANES), lambda t: (t, 0)),
            pl.BlockSpec((PROJ_TM, LANES), lambda t: (t, 0)),
        ],
        out_specs=pl.BlockSpec((PROJ_TM, RET_IN_WIDTH), lambda t: (t, 0)),
        out_shape=jax.ShapeDtypeStruct((TOKENS, RET_IN_WIDTH), bf16),
        compiler_params=_params("parallel"),
        name="ret_proj",
    )(x2d, gain.reshape(1, D_MODEL), w_bf16, cos, sin)


def _ret_core_kernel(cd_ref, q_ref, k_ref, v_ref, g_ref, din_ref, qdec_ref, kdec_ref, gn_ref, y_ref, r_ref):
    t = pl.program_id(1)
    dk, dv = RET_KEY_DIM, RET_VAL_DIM

    @pl.when(t == 0)
    def _():
        r_ref[...] = jnp.zeros_like(r_ref)

    for h in range(RET_HEADS):
        qn = q_ref[:, h * dk:(h + 1) * dk]
        kn = k_ref[:, h * dk:(h + 1) * dk]
        vn = v_ref[:, h * dv:(h + 1) * dv]
        sc = lax.dot_general(qn, kn, NT_DIMS, preferred_element_type=f32) * din_ref[h]
        r_old = r_ref[h]
        qd = (qn.astype(f32) * qdec_ref[h]).astype(bf16)
        o = jnp.dot(jnp.concatenate([sc.astype(bf16), qd], axis=1),
                    jnp.concatenate([vn, r_old.astype(bf16)], axis=0), preferred_element_type=f32)
        kdt = (kn.astype(f32) * kdec_ref[h]).T.astype(bf16)
        r_ref[h] = cd_ref[h] * r_old + jnp.dot(kdt, vn, preferred_element_type=f32)
        mu = jnp.mean(o, axis=-1, keepdims=True)
        d = o - mu
        var = jnp.mean(d * d, axis=-1, keepdims=True)
        on = d * lax.rsqrt(var + EPS) * gn_ref[:, h * dv:(h + 1) * dv]
        y_ref[:, h * dv:(h + 1) * dv] = (_silu(g_ref[:, h * dv:(h + 1) * dv].astype(f32)) * on).astype(bf16)


def _ret_core(proj3d, gn_g, decay_in, q_decay, k_decay, chunk_decay):
    nt = SEQ // RET_TQ
    H, dk, dv, C = RET_HEADS, RET_KEY_DIM, RET_VAL_DIM, RET_TQ
    assert 2 * H * dk == H * dv
    resident = dict(pipeline_mode=pl.Buffered(1))
    return pl.pallas_call(
        _ret_core_kernel,
        grid=(BATCH, nt),
        in_specs=[
            pl.BlockSpec(memory_space=pltpu.SMEM),
            pl.BlockSpec((None, RET_TQ, H * dk), lambda b, t: (b, t, 0)),
            pl.BlockSpec((None, RET_TQ, H * dk), lambda b, t: (b, t, 1)),
            pl.BlockSpec((None, RET_TQ, H * dv), lambda b, t: (b, t, 1)),
            pl.BlockSpec((None, RET_TQ, H * dv), lambda b, t: (b, t, 2)),
            pl.BlockSpec((H, C, C), lambda b, t: (0, 0, 0), **resident),
            pl.BlockSpec((H, C, 1), lambda b, t: (0, 0, 0)),
            pl.BlockSpec((H, C, 1), lambda b, t: (0, 0, 0)),
            pl.BlockSpec((1, H * dv), lambda b, t: (0, 0)),
        ],
        out_specs=pl.BlockSpec((None, RET_TQ, H * dv), lambda b, t: (b, t, 0)),
        out_shape=jax.ShapeDtypeStruct((BATCH, SEQ, H * dv), bf16),
        scratch_shapes=[pltpu.VMEM((H, dk, dv), f32)],
        compiler_params=_params("parallel", "arbitrary"),
        name="ret_core",
    )(chunk_decay, proj3d, proj3d, proj3d, proj3d, decay_in, q_decay, k_decay, gn_g.reshape(1, H * dv))


def _ffn_kernel(y_ref, wo_ref, x_ref, g_ref, wa_ref, wb_ref, cw_ref, cb_ref, wd_ref, fg_ref, o_ref,
                aprev_ref, h_ref, hid_ref, *, final_norm):
    t = pl.program_id(0)

    @pl.when(t == 0)
    def _():
        aprev_ref[...] = jnp.zeros_like(aprev_ref)

    x1 = x_ref[...] + jnp.dot(y_ref[...], wo_ref[...], preferred_element_type=f32)
    o_ref[...] = x1
    h_ref[...] = _rms_norm(x1, g_ref[...]).astype(bf16)
    seq_start = (t * FFN_TM) % SEQ == 0
    for c0 in range(0, D_FF, FFN_CHUNK):
        cs = slice(c0, min(c0 + FFN_CHUNK, D_FF))
        a = jnp.dot(h_ref[...], wa_ref[:, cs], preferred_element_type=f32)
        b = jnp.dot(h_ref[...], wb_ref[:, cs], preferred_element_type=f32)
        prev = jnp.where(seq_start, 0.0, aprev_ref[:, cs])
        aprev_ref[:, cs] = a[FFN_TM - FFN_HALO:, :]
        a_full = jnp.concatenate([prev, a], axis=0)
        a1 = pltpu.roll(a_full, 1, 0)[FFN_HALO:]
        a2 = pltpu.roll(a_full, 2, 0)[FFN_HALO:]
        cw = cw_ref[:, cs]
        conv = cw[0:1] * a2 + cw[1:2] * a1 + cw[2:3] * a + cb_ref[:, cs]
        hid_ref[:, cs] = (_silu(conv) * b).astype(bf16)
    out = o_ref[...] + jnp.dot(hid_ref[...], wd_ref[...], preferred_element_type=f32)
    o_ref[...] = _rms_norm(out, fg_ref[...]) if final_norm else out


def _ffn(y2d, w_out, x2d, gain, wa, wb, conv_w, conv_b, wd, final_gain):
    kdim = y2d.shape[1]
    nt = TOKENS // FFN_TM
    resident = dict(pipeline_mode=pl.Buffered(1))
    final_norm = final_gain is not None
    fg = final_gain if final_norm else gain
    return pl.pallas_call(
        functools.partial(_ffn_kernel, final_norm=final_norm),
        grid=(nt,),
        in_specs=[
            pl.BlockSpec((FFN_TM, kdim), lambda t: (t, 0)),
            pl.BlockSpec((kdim, D_MODEL), lambda t: (0, 0), **resident),
            pl.BlockSpec((FFN_TM, D_MODEL), lambda t: (t, 0)),
            pl.BlockSpec((1, D_MODEL), lambda t: (0, 0)),
            pl.BlockSpec((D_MODEL, D_FF), lambda t: (0, 0), **resident),
            pl.BlockSpec((D_MODEL, D_FF), lambda t: (0, 0), **resident),
            pl.BlockSpec((CONV_WIDTH, D_FF), lambda t: (0, 0)),
            pl.BlockSpec((1, D_FF), lambda t: (0, 0)),
            pl.BlockSpec((D_FF, D_MODEL), lambda t: (0, 0), **resident),
            pl.BlockSpec((1, D_MODEL), lambda t: (0, 0)),
        ],
        out_specs=pl.BlockSpec((FFN_TM, D_MODEL), lambda t: (t, 0)),
        out_shape=jax.ShapeDtypeStruct((TOKENS, D_MODEL), f32),
        scratch_shapes=[
            pltpu.VMEM((FFN_HALO, D_FF), f32),
            pltpu.VMEM((FFN_TM, D_MODEL), bf16),
            pltpu.VMEM((FFN_TM, D_FF), bf16),
        ],
        compiler_params=_params("arbitrary"),
        name="out_proj_conv_ffn",
    )(y2d, w_out, x2d, gain.reshape(1, D_MODEL), wa, wb, conv_w, conv_b.reshape(1, D_FF), wd, fg.reshape(1, D_MODEL))


def _rope_angles(positions, dim):
    inv = 1.0 / (ROPE_THETA ** (jnp.arange(0, dim, 2, dtype=f32) / dim))
    return positions.astype(f32)[..., None] * inv


def _head_rope_tables(positions):
    ang = _rope_angles(positions, HEAD_DIM)
    cos, sin = jnp.cos(ang), jnp.sin(ang)
    reps = LANES // HEAD_DIM
    cosf = jnp.tile(jnp.concatenate([cos, cos], axis=-1), (1, 1, reps))
    sins = jnp.tile(jnp.concatenate([-sin, sin], axis=-1), (1, 1, reps))
    return cosf, sins


def _retention_decays():
    H, C = RET_HEADS, RET_TQ
    log_gamma = jnp.log(1.0 - 2.0 ** (-5.0 - jnp.arange(H, dtype=f32)))
    idx = jnp.arange(C, dtype=f32)
    diff = idx[:, None] - idx[None, :]
    decay_in = jnp.where(diff[None] >= 0, jnp.exp(jnp.maximum(diff, 0.0)[None] * log_gamma[:, None, None]), 0.0)
    q_decay = jnp.exp((idx + 1.0)[None, :] * log_gamma[:, None])
    k_decay = jnp.exp((C - 1.0 - idx)[None, :] * log_gamma[:, None])
    chunk_decay = jnp.exp(C * log_gamma)
    return decay_in, q_decay[..., None], k_decay[..., None], chunk_decay


def kernel(x, positions, l0_attn_norm, l0_w_in, l0_w_out, l0_sinks, l0_ffn_norm, l0_w_a, l0_w_b, l0_conv_w, l0_conv_b, l0_w_down, l1_attn_norm, l1_w_in, l1_w_out, l1_gn_g, l1_ffn_norm, l1_w_a, l1_w_b, l1_conv_w, l1_conv_b, l1_w_down, l2_attn_norm, l2_w_in, l2_w_out, l2_ffn_norm, l2_w_a, l2_w_b, l2_conv_w, l2_conv_b, l2_w_down, l3_attn_norm, l3_w_in, l3_w_out, l3_sinks, l3_ffn_norm, l3_w_a, l3_w_b, l3_conv_w, l3_conv_b, l3_w_down, final_norm):
    mixers = [
        (l0_attn_norm, l0_w_in, l0_w_out, l0_sinks),
        (l1_attn_norm, l1_w_in, l1_w_out, l1_gn_g),
        (l2_attn_norm, l2_w_in, l2_w_out, None),
        (l3_attn_norm, l3_w_in, l3_w_out, l3_sinks),
    ]
    ffns = [
        (l0_ffn_norm, l0_w_a, l0_w_b, l0_conv_w, l0_conv_b, l0_w_down),
        (l1_ffn_norm, l1_w_a, l1_w_b, l1_conv_w, l1_conv_b, l1_w_down),
        (l2_ffn_norm, l2_w_a, l2_w_b, l2_conv_w, l2_conv_b, l2_w_down),
        (l3_ffn_norm, l3_w_a, l3_w_b, l3_conv_w, l3_conv_b, l3_w_down),
    ]
    cosf, sins = _head_rope_tables(positions)
    ang_r = _rope_angles(positions, RET_KEY_DIM).reshape(TOKENS, RET_KEY_DIM // 2)
    cos_r, sin_r = jnp.cos(ang_r), jnp.sin(ang_r)
    decay_in, q_decay, k_decay, chunk_decay = _retention_decays()

    x2d = x.reshape(TOKENS, D_MODEL)
    for i in range(DEPTH):
        norm_g, w_in, w_out, extra = mixers[i]
        m = i % N_MIXERS
        w_in_b = w_in.astype(bf16)
        if m == 1:
            proj = _ret_proj(x2d, norm_g, w_in_b, cos_r, sin_r)
            y = _ret_core(proj.reshape(BATCH, SEQ, RET_IN_WIDTH), extra, decay_in, q_decay, k_decay, chunk_decay)
            y2d = y.reshape(TOKENS, RET_HEADS * RET_VAL_DIM)
        else:
            q, k, v = _attn_proj(x2d.reshape(BATCH, SEQ, D_MODEL), norm_g, w_in_b, cosf, sins)
            y = _swa(q, k, v, extra) if m == 0 else _moba(q, k, v)
            y2d = y.reshape(TOKENS, D_MODEL)
        fg, wa, wb, cw, cb, wd = ffns[i]
        x2d = _ffn(y2d, w_out.astype(bf16), x2d, fg, wa.astype(bf16), wb.astype(bf16), cw, cb, wd.astype(bf16),
                   final_norm if i == DEPTH - 1 else None)
    return x2d.reshape(BATCH, SEQ, D_MODEL)
```

```python
import functools

import jax
import jax.numpy as jnp
from jax import lax
from jax.experimental import pallas as pl
from jax.experimental.pallas import tpu as pltpu

D_MODEL = 1024
BATCH = 4
SEQ = 4096
DEPTH = 4
N_MIXERS = 3
HEAD_DIM = 64
N_Q_HEADS = D_MODEL // HEAD_DIM
N_KV_HEADS = 4
GQA_GROUP = N_Q_HEADS // N_KV_HEADS
QKV_WIDTH = (N_Q_HEADS + 2 * N_KV_HEADS) * HEAD_DIM
ROPE_THETA = 10000.0
SWA_WINDOW = 128
RET_HEADS = 4
RET_KEY_DIM = D_MODEL // RET_HEADS
RET_VAL_DIM = 2 * RET_KEY_DIM
RET_IN_WIDTH = 2 * D_MODEL + 2 * RET_HEADS * RET_VAL_DIM
RET_CHUNK = 128
MOBA_BLOCK = 256
MOBA_TOPK = 3
D_FF = ((8 * D_MODEL // 3 + 127) // 128) * 128
CONV_WIDTH = 3
EPS = 1e-6
NEG_INF = -1e30

LANES = 128
SUBLANES = 8
MXU_WIDTH = 256
VMEM_LIMIT_BYTES = 56 * 1024 * 1024

TOKENS = BATCH * SEQ
PROJ_TM = 512
ATT_TQ = 256
RET_TQ = 512
MOBA_ROW_CHUNK = 128
MOBA_GROUPS = 4
FFN_TM = 512
FFN_CHUNK = 256
FFN_HALO = SUBLANES

f32 = jnp.float32
bf16 = jnp.bfloat16
NT_DIMS = (((1,), (1,)), ((), ()))


def _params(*semantics):
    return pltpu.CompilerParams(dimension_semantics=semantics, vmem_limit_bytes=VMEM_LIMIT_BYTES)


def _rms_norm(x, gain):
    ms = jnp.mean(x * x, axis=-1, keepdims=True)
    return x * lax.rsqrt(ms + EPS) * gain


def _silu(x):
    half = 0.5 * x
    return half + half * jnp.tanh(half)


def _normalize_pair(pv_even, pv_odd, extra_even, extra_odd):
    low = lax.broadcasted_iota(jnp.int32, pv_even.shape, 1) < HEAD_DIM
    num = jnp.where(low, pv_even, pltpu.roll(pv_odd, HEAD_DIM, 1))
    den = jnp.where(low, pltpu.roll(pv_even, HEAD_DIM, 1) + extra_even, pv_odd + extra_odd)
    return num / den


def _attn_proj_kernel(x_ref, g_ref, w_ref, cos_ref, sin_ref, q_ref, k_ref, v_ref):
    t = pl.program_id(1)
    h = _rms_norm(x_ref[...], g_ref[...]).astype(bf16)
    cosf = cos_ref[...]
    sins = sin_ref[...]
    lane = lax.broadcasted_iota(jnp.int32, (PROJ_TM, LANES), 1)
    first_half = (lane % HEAD_DIM) < (HEAD_DIM // 2)
    low = lane < HEAD_DIM
    blk = (t * PROJ_TM + lax.broadcasted_iota(jnp.int32, (PROJ_TM, LANES), 0)) // MOBA_BLOCK
    k_tail = jnp.where(lane == HEAD_DIM + blk, 1.0, 0.0)
    v_tail = 1.0

    def split(r, tail):
        return (jnp.where(low, r, tail).astype(bf16),
                jnp.where(low, pltpu.roll(r, HEAD_DIM, 1), tail).astype(bf16))

    def rope(xc):
        partner = jnp.where(first_half, pltpu.roll(xc, LANES - HEAD_DIM // 2, 1), pltpu.roll(xc, HEAD_DIM // 2, 1))
        return xc * cosf + partner * sins

    n_q = N_Q_HEADS * HEAD_DIM // LANES
    n_k = N_KV_HEADS * HEAD_DIM // LANES
    groups_per_dot = MXU_WIDTH // LANES
    for d in range(QKV_WIDTH // MXU_WIDTH):
        proj = jnp.dot(h, w_ref[:, d * MXU_WIDTH:(d + 1) * MXU_WIDTH], preferred_element_type=f32)
        for e in range(groups_per_dot):
            c = d * groups_per_dot + e
            xc = proj[:, e * LANES:(e + 1) * LANES]
            if c < n_q:
                q_ref[2 * c], q_ref[2 * c + 1] = split(rope(xc) * (HEAD_DIM ** -0.5), 0.0)
            elif c < n_q + n_k:
                k_ref[2 * (c - n_q)], k_ref[2 * (c - n_q) + 1] = split(rope(xc), k_tail)
            else:
                v_ref[2 * (c - n_q - n_k)], v_ref[2 * (c - n_q - n_k) + 1] = split(xc, v_tail)


def _attn_proj(x, gain, w_bf16, cosf, sins):
    nt = SEQ // PROJ_TM
    return pl.pallas_call(
        _attn_proj_kernel,
        grid=(BATCH, nt),
        in_specs=[
            pl.BlockSpec((None, PROJ_TM, D_MODEL), lambda b, t: (b, t, 0)),
            pl.BlockSpec((1, D_MODEL), lambda b, t: (0, 0)),
            pl.BlockSpec((D_MODEL, QKV_WIDTH), lambda b, t: (0, 0)),
            pl.BlockSpec((None, PROJ_TM, LANES), lambda b, t: (b, t, 0)),
            pl.BlockSpec((None, PROJ_TM, LANES), lambda b, t: (b, t, 0)),
        ],
        out_specs=[
            pl.BlockSpec((None, N_Q_HEADS, PROJ_TM, LANES), lambda b, t: (b, 0, t, 0)),
            pl.BlockSpec((None, N_KV_HEADS, PROJ_TM, LANES), lambda b, t: (b, 0, t, 0)),
            pl.BlockSpec((None, N_KV_HEADS, PROJ_TM, LANES), lambda b, t: (b, 0, t, 0)),
        ],
        out_shape=[
            jax.ShapeDtypeStruct((BATCH, N_Q_HEADS, SEQ, LANES), bf16),
            jax.ShapeDtypeStruct((BATCH, N_KV_HEADS, SEQ, LANES), bf16),
            jax.ShapeDtypeStruct((BATCH, N_KV_HEADS, SEQ, LANES), bf16),
        ],
        compiler_params=_params("parallel", "parallel"),
        name="attn_proj",
    )(x, gain.reshape(1, D_MODEL), w_bf16, cosf, sins)


def _swa_kernel(sinks_ref, q_ref, k_ref, v_ref, o_ref, s_ref):
    g = pl.program_id(1)
    t = pl.program_id(2)
    W = SWA_WINDOW
    nsub = ATT_TQ // W
    qi = lax.broadcasted_iota(jnp.int32, (W, 2 * W), 0)
    col = lax.broadcasted_iota(jnp.int32, (W, 2 * W), 1)
    band = jnp.where((col > qi) & (col <= qi + W), 0.0, NEG_INF)
    first = jnp.where(col <= qi, 0.0, NEG_INF)

    def key_start(c):
        return pl.multiple_of(jnp.maximum(t * ATT_TQ + (c - 1) * W, 0), W)

    def scores(c):
        q4 = q_ref[:, c * W:(c + 1) * W, :].reshape(GQA_GROUP * W, LANES)
        kk = k_ref[pl.ds(key_start(c), 2 * W), :]
        s_ref[c] = lax.dot_general(q4, kk, NT_DIMS, preferred_element_type=f32)

    def head(c, j, vv, bias):
        sink = sinks_ref[g * GQA_GROUP + j]
        s = s_ref[c, j * W:(j + 1) * W, :] + bias
        s0, s1 = s[:, :LANES], s[:, LANES:]
        m = jnp.maximum(jnp.max(jnp.maximum(s0, s1), axis=1, keepdims=True), sink)
        p = jnp.concatenate([jnp.exp(s0 - m), jnp.exp(s1 - m)], axis=1).astype(bf16)
        return jnp.dot(p, vv, preferred_element_type=f32), jnp.exp(sink - m)

    scores(0)
    for c in range(nsub):
        if c + 1 < nsub:
            scores(c + 1)
        vv = v_ref[pl.ds(key_start(c), 2 * W), :]
        bias = jnp.where(t == 0, first, band) if c == 0 else band
        for jj in range(GQA_GROUP // 2):
            pv0, e0 = head(c, 2 * jj, vv, bias)
            pv1, e1 = head(c, 2 * jj + 1, vv, bias)
            o_ref[c * W:(c + 1) * W, jj * LANES:(jj + 1) * LANES] = _normalize_pair(pv0, pv1, e0, e1).astype(bf16)


def _swa(q, k, v, sinks):
    nt = SEQ // ATT_TQ
    gw = GQA_GROUP * HEAD_DIM
    return pl.pallas_call(
        _swa_kernel,
        grid=(BATCH, N_KV_HEADS, nt),
        in_specs=[
            pl.BlockSpec(memory_space=pltpu.SMEM),
            pl.BlockSpec((None, GQA_GROUP, ATT_TQ, LANES), lambda b, g, t: (b, g, t, 0)),
            pl.BlockSpec((None, None, SEQ, LANES), lambda b, g, t: (b, g, 0, 0)),
            pl.BlockSpec((None, None, SEQ, LANES), lambda b, g, t: (b, g, 0, 0)),
        ],
        out_specs=pl.BlockSpec((None, ATT_TQ, gw), lambda b, g, t: (b, t, g)),
        out_shape=jax.ShapeDtypeStruct((BATCH, SEQ, D_MODEL), bf16),
        scratch_shapes=[pltpu.VMEM((ATT_TQ // SWA_WINDOW, GQA_GROUP * SWA_WINDOW, 2 * SWA_WINDOW), f32)],
        compiler_params=_params("parallel", "parallel", "parallel"),
        name="swa_core",
    )(sinks, q, k, v)


def _moba_kernel(q_ref, k_ref, v_ref, o_ref, kmean_ref, qa_ref, sa_ref, sb_ref, m_ref, acc_ref):
    i = pl.program_id(2)
    BS = MOBA_BLOCK
    nb = SEQ // BS
    G = MOBA_GROUPS
    grows = GQA_GROUP * BS
    rows = G * grows

    @pl.when(i == 0)
    def _():
        lane = lax.broadcasted_iota(jnp.int32, (1, LANES), 1)
        for g in range(G):
            for n in range(nb):
                kb = k_ref[g, n * BS:(n + 1) * BS, :].astype(f32)
                kmean_ref[g, n:n + 1, :] = jnp.where(lane < HEAD_DIM, jnp.mean(kb, axis=0, keepdims=True), 0.0)

    q = q_ref[...].reshape(rows, LANES)
    gate_t = jnp.concatenate(
        [lax.dot_general(kmean_ref[g].astype(bf16), q[g * grows:(g + 1) * grows], NT_DIMS,
                         preferred_element_type=f32) for g in range(G)], axis=1)
    blk = lax.broadcasted_iota(jnp.int32, (nb, rows), 0)
    past = blk < i
    gm = jnp.where(past, gate_t, NEG_INF)
    sel = jnp.zeros((nb, rows), jnp.int32)
    for _ in range(MOBA_TOPK):
        mx = jnp.max(gm, axis=0, keepdims=True)
        cand = (gm == mx) & past & (sel == 0)
        idx = jnp.min(jnp.where(cand, blk, nb), axis=0, keepdims=True)
        pick = blk == idx
        sel = jnp.where(pick, 1, sel)
        gm = jnp.where(pick, NEG_INF, gm)
    bias_t = jnp.where((sel == 1) | jnp.logical_not(past), 0.0, NEG_INF)
    bias_pad_t = jnp.concatenate(
        [jnp.zeros((HEAD_DIM, rows), f32), bias_t, jnp.zeros((LANES - HEAD_DIM - nb, rows), f32)], axis=0)
    qa_ref[...] = (q.astype(f32) + bias_pad_t.T).astype(bf16)

    RC = MOBA_ROW_CHUNK
    qi = lax.broadcasted_iota(jnp.int32, (RC, LANES), 0)
    kj = lax.broadcasted_iota(jnp.int32, (RC, LANES), 1)

    def block_start(n):
        return pl.multiple_of(jnp.minimum(n, nb - 1) * BS, BS)

    def plain_rows(g):
        return q_ref[g * GQA_GROUP:(g + 1) * GQA_GROUP].reshape(grows, LANES)

    def biased_rows(g):
        return qa_ref[g * grows:(g + 1) * grows, :]

    def scores(g, n, s_ref, lhs_rows=biased_rows):
        k_n = k_ref[g, pl.ds(block_start(n), BS), :]
        s_ref[g * grows:(g + 1) * grows, :] = lax.dot_general(lhs_rows(g), k_n, NT_DIMS, preferred_element_type=f32)

    def softmax_pv(g, n, s_ref, own):
        v_n = v_ref[g, pl.ds(block_start(n), BS), :]
        for c in range(grows // RC):
            rs = slice(g * grows + c * RC, g * grows + (c + 1) * RC)
            s = s_ref[rs, :]
            s0, s1 = s[:, :LANES], s[:, LANES:]
            if own:
                q0 = (c * RC) % BS
                s0 = jnp.where(kj <= qi + q0, s0, NEG_INF)
                s1 = jnp.where(kj + LANES <= qi + q0, s1, NEG_INF)
            row_max = jnp.max(jnp.maximum(s0, s1), axis=1, keepdims=True)
            if own:
                m_new = jnp.broadcast_to(row_max, (RC, LANES))
            else:
                m_old = m_ref[rs, :]
                m_new = jnp.maximum(m_old, row_max)
            p = jnp.concatenate([jnp.exp(s0 - m_new), jnp.exp(s1 - m_new)], axis=1).astype(bf16)
            pv = jnp.dot(p, v_n, preferred_element_type=f32)
            if own:
                acc_ref[rs, :] = pv
            else:
                acc_ref[rs, :] = jnp.exp(m_old - m_new) * acc_ref[rs, :] + pv
            m_ref[rs, :] = m_new

    def step(n_next, next_ref, n_cur, cur_ref, own):
        for g in range(G):
            scores(g, n_next, next_ref)
            softmax_pv(g, n_cur, cur_ref, own)

    for g in range(G):
        scores(g, i, sa_ref, plain_rows)
    step(0, sb_ref, i, sa_ref, True)

    def body(n, carry):
        @pl.when(n % 2 == 0)
        def _():
            step(n + 1, sa_ref, n, sb_ref, False)

        @pl.when(n % 2 == 1)
        def _():
            step(n + 1, sb_ref, n, sa_ref, False)

        return carry

    lax.fori_loop(0, i, body, 0)
    for jj in range(G * GQA_GROUP // 2):
        a0 = acc_ref[2 * jj * BS:(2 * jj + 1) * BS, :]
        a1 = acc_ref[(2 * jj + 1) * BS:(2 * jj + 2) * BS, :]
        o_ref[:, jj * LANES:(jj + 1) * LANES] = _normalize_pair(a0, a1, 0.0, 0.0).astype(bf16)


def _moba(q, k, v):
    nb = SEQ // MOBA_BLOCK
    G = MOBA_GROUPS
    gw = G * GQA_GROUP * HEAD_DIM
    rows = G * GQA_GROUP * MOBA_BLOCK
    return pl.pallas_call(
        _moba_kernel,
        grid=(BATCH, N_KV_HEADS // G, nb),
        in_specs=[
            pl.BlockSpec((None, G * GQA_GROUP, MOBA_BLOCK, LANES), lambda b, g, i: (b, g, i, 0)),
            pl.BlockSpec((None, G, SEQ, LANES), lambda b, g, i: (b, g, 0, 0)),
            pl.BlockSpec((None, G, SEQ, LANES), lambda b, g, i: (b, g, 0, 0)),
        ],
        out_specs=pl.BlockSpec((None, MOBA_BLOCK, gw), lambda b, g, i: (b, i, g)),
        out_shape=jax.ShapeDtypeStruct((BATCH, SEQ, D_MODEL), bf16),
        scratch_shapes=[
            pltpu.VMEM((G, nb, LANES), f32),
            pltpu.VMEM((rows, LANES), bf16),
            pltpu.VMEM((rows, MOBA_BLOCK), f32),
            pltpu.VMEM((rows, MOBA_BLOCK), f32),
            pltpu.VMEM((rows, LANES), f32),
            pltpu.VMEM((rows, LANES), f32),
        ],
        compiler_params=_params("parallel", "parallel", "arbitrary"),
        name="moba_core",
    )(q, k, v)


def _ret_proj_kernel(x_ref, g_ref, w_ref, cos_ref, sin_ref, o_ref):
    h = _rms_norm(x_ref[...], g_ref[...]).astype(bf16)
    cos = cos_ref[...]
    sin = sin_ref[...]
    half = RET_KEY_DIM // 2
    n_rope = 2 * RET_HEADS
    for c in range(RET_IN_WIDTH // RET_KEY_DIM):
        c0 = c * RET_KEY_DIM
        proj = jnp.dot(h, w_ref[:, c0:c0 + RET_KEY_DIM], preferred_element_type=f32)
        if c < n_rope:
            x1, x2 = proj[:, :half], proj[:, half:]
            r1, r2 = x1 * cos - x2 * sin, x2 * cos + x1 * sin
            if c >= RET_HEADS:
                r1, r2 = r1 * (RET_KEY_DIM ** -0.5), r2 * (RET_KEY_DIM ** -0.5)
            o_ref[:, c0:c0 + half] = r1.astype(bf16)
            o_ref[:, c0 + half:c0 + RET_KEY_DIM] = r2.astype(bf16)
        else:
            o_ref[:, c0:c0 + RET_KEY_DIM] = proj.astype(bf16)


def _ret_proj(x2d, gain, w_bf16, cos, sin):
    nt = TOKENS // PROJ_TM
    return pl.pallas_call(
        _ret_proj_kernel,
        grid=(nt,),
        in_specs=[
            pl.BlockSpec((PROJ_TM, D_MODEL), lambda t: (t, 0)),
            pl.BlockSpec((1, D_MODEL), lambda t: (0, 0)),
            pl.BlockSpec((D_MODEL, RET_IN_WIDTH), lambda t: (0, 0), pipeline_mode=pl.Buffered(1)),
            pl.BlockSpec((PROJ_TM, LANES), lambda t: (t, 0)),
            pl.BlockSpec((PROJ_TM, LANES), lambda t: (t, 0)),
        ],
        out_specs=pl.BlockSpec((PROJ_TM, RET_IN_WIDTH), lambda t: (t, 0)),
        out_shape=jax.ShapeDtypeStruct((TOKENS, RET_IN_WIDTH), bf16),
        compiler_params=_params("parallel"),
        name="ret_proj",
    )(x2d, gain.reshape(1, D_MODEL), w_bf16, cos, sin)


def _ret_core_kernel(cd_ref, q_ref, k_ref, v_ref, g_ref, din_ref, qdec_ref, kdec_ref, gn_ref, y_ref, r_ref):
    t = pl.program_id(1)
    dk, dv = RET_KEY_DIM, RET_VAL_DIM

    @pl.when(t == 0)
    def _():
        r_ref[...] = jnp.zeros_like(r_ref)

    for h in range(RET_HEADS):
        qn = q_ref[:, h * dk:(h + 1) * dk]
        kn = k_ref[:, h * dk:(h + 1) * dk]
        vn = v_ref[:, h * dv:(h + 1) * dv]
        sc = lax.dot_general(qn, kn, NT_DIMS, preferred_element_type=f32) * din_ref[h]
        r_old = r_ref[h]
        qd = (qn.astype(f32) * qdec_ref[h]).astype(bf16)
        o = jnp.dot(jnp.concatenate([sc.astype(bf16), qd], axis=1),
                    jnp.concatenate([vn, r_old.astype(bf16)], axis=0), preferred_element_type=f32)
        kdt = (kn.astype(f32) * kdec_ref[h]).T.astype(bf16)
        r_ref[h] = cd_ref[h] * r_old + jnp.dot(kdt, vn, preferred_element_type=f32)
        mu = jnp.mean(o, axis=-1, keepdims=True)
        d = o - mu
        var = jnp.mean(d * d, axis=-1, keepdims=True)
        on = d * lax.rsqrt(var + EPS) * gn_ref[:, h * dv:(h + 1) * dv]
        y_ref[:, h * dv:(h + 1) * dv] = (_silu(g_ref[:, h * dv:(h + 1) * dv].astype(f32)) * on).astype(bf16)


def _ret_core(proj3d, gn_g, decay_in, q_decay, k_decay, chunk_decay):
    nt = SEQ // RET_TQ
    H, dk, dv, C = RET_HEADS, RET_KEY_DIM, RET_VAL_DIM, RET_TQ
    assert 2 * H * dk == H * dv
    resident = dict(pipeline_mode=pl.Buffered(1))
    return pl.pallas_call(
        _ret_core_kernel,
        grid=(BATCH, nt),
        in_specs=[
            pl.BlockSpec(memory_space=pltpu.SMEM),
            pl.BlockSpec((None, RET_TQ, H * dk), lambda b, t: (b, t, 0)),
            pl.BlockSpec((None, RET_TQ, H * dk), lambda b, t: (b, t, 1)),
            pl.BlockSpec((None, RET_TQ, H * dv), lambda b, t: (b, t, 1)),
            pl.BlockSpec((None, RET_TQ, H * dv), lambda b, t: (b, t, 2)),
            pl.BlockSpec((H, C, C), lambda b, t: (0, 0, 0), **resident),
            pl.BlockSpec((H, C, 1), lambda b, t: (0, 0, 0)),
            pl.BlockSpec((H, C, 1), lambda b, t: (0, 0, 0)),
            pl.BlockSpec((1, H * dv), lambda b, t: (0, 0)),
        ],
        out_specs=pl.BlockSpec((None, RET_TQ, H * dv), lambda b, t: (b, t, 0)),
        out_shape=jax.ShapeDtypeStruct((BATCH, SEQ, H * dv), bf16),
        scratch_shapes=[pltpu.VMEM((H, dk, dv), f32)],
        compiler_params=_params("parallel", "arbitrary"),
        name="ret_core",
    )(chunk_decay, proj3d, proj3d, proj3d, proj3d, decay_in, q_decay, k_decay, gn_g.reshape(1, H * dv))


def _ffn_kernel(y_ref, wo_ref, x_ref, g_ref, wa_ref, wb_ref, cw_ref, cb_ref, wd_ref, fg_ref, o_ref,
                aprev_ref, h_ref, hid_ref, *, final_norm):
    t = pl.program_id(0)

    @pl.when(t == 0)
    def _():
        aprev_ref[...] = jnp.zeros_like(aprev_ref)

    x1 = x_ref[...] + jnp.dot(y_ref[...], wo_ref[...], preferred_element_type=f32)
    o_ref[...] = x1
    h_ref[...] = _rms_norm(x1, g_ref[...]).astype(bf16)
    seq_start = (t * FFN_TM) % SEQ == 0
    for c0 in range(0, D_FF, FFN_CHUNK):
        cs = slice(c0, min(c0 + FFN_CHUNK, D_FF))
        a = jnp.dot(h_ref[...], wa_ref[:, cs], preferred_element_type=f32)
        b = jnp.dot(h_ref[...], wb_ref[:, cs], preferred_element_type=f32)
        prev = jnp.where(seq_start, 0.0, aprev_ref[:, cs])
        aprev_ref[:, cs] = a[FFN_TM - FFN_HALO:, :]
        a_full = jnp.concatenate([prev, a], axis=0)
        a1 = pltpu.roll(a_full, 1, 0)[FFN_HALO:]
        a2 = pltpu.roll(a_full, 2, 0)[FFN_HALO:]
        cw = cw_ref[:, cs]
        conv = cw[0:1] * a2 + cw[1:2] * a1 + cw[2:3] * a + cb_ref[:, cs]
        hid_ref[:, cs] = (_silu(conv) * b).astype(bf16)
    out = o_ref[...] + jnp.dot(hid_ref[...], wd_ref[...], preferred_element_type=f32)
    o_ref[...] = _rms_norm(out, fg_ref[...]) if final_norm else out


def _ffn(y2d, w_out, x2d, gain, wa, wb, conv_w, conv_b, wd, final_gain):
    kdim = y2d.shape[1]
    nt = TOKENS // FFN_TM
    resident = dict(pipeline_mode=pl.Buffered(1))
    final_norm = final_gain is not None
    fg = final_gain if final_norm else gain
    return pl.pallas_call(
        functools.partial(_ffn_kernel, final_norm=final_norm),
        grid=(nt,),
        in_specs=[
            pl.BlockSpec((FFN_TM, kdim), lambda t: (t, 0)),
            pl.BlockSpec((kdim, D_MODEL), lambda t: (0, 0), **resident),
            pl.BlockSpec((FFN_TM, D_MODEL), lambda t: (t, 0)),
            pl.BlockSpec((1, D_MODEL), lambda t: (0, 0)),
            pl.BlockSpec((D_MODEL, D_FF), lambda t: (0, 0), **resident),
            pl.BlockSpec((D_MODEL, D_FF), lambda t: (0, 0), **resident),
            pl.BlockSpec((CONV_WIDTH, D_FF), lambda t: (0, 0)),
            pl.BlockSpec((1, D_FF), lambda t: (0, 0)),
            pl.BlockSpec((D_FF, D_MODEL), lambda t: (0, 0), **resident),
            pl.BlockSpec((1, D_MODEL), lambda t: (0, 0)),
        ],
        out_specs=pl.BlockSpec((FFN_TM, D_MODEL), lambda t: (t, 0)),
        out_shape=jax.ShapeDtypeStruct((TOKENS, D_MODEL), f32),
        scratch_shapes=[
            pltpu.VMEM((FFN_HALO, D_FF), f32),
            pltpu.VMEM((FFN_TM, D_MODEL), bf16),
            pltpu.VMEM((FFN_TM, D_FF), bf16),
        ],
        compiler_params=_params("arbitrary"),
        name="out_proj_conv_ffn",
    )(y2d, w_out, x2d, gain.reshape(1, D_MODEL), wa, wb, conv_w, conv_b.reshape(1, D_FF), wd, fg.reshape(1, D_MODEL))


def _rope_angles(positions, dim):
    inv = 1.0 / (ROPE_THETA ** (jnp.arange(0, dim, 2, dtype=f32) / dim))
    return positions.astype(f32)[..., None] * inv


def _head_rope_tables(positions):
    ang = _rope_angles(positions, HEAD_DIM)
    cos, sin = jnp.cos(ang), jnp.sin(ang)
    reps = LANES // HEAD_DIM
    cosf = jnp.tile(jnp.concatenate([cos, cos], axis=-1), (1, 1, reps))
    sins = jnp.tile(jnp.concatenate([-sin, sin], axis=-1), (1, 1, reps))
    return cosf, sins


def _retention_decays():
    H, C = RET_HEADS, RET_TQ
    log_gamma = jnp.log(1.0 - 2.0 ** (-5.0 - jnp.arange(H, dtype=f32)))
    idx = jnp.arange(C, dtype=f32)
    diff = idx[:, None] - idx[None, :]
    decay_in = jnp.where(diff[None] >= 0, jnp.exp(jnp.maximum(diff, 0.0)[None] * log_gamma[:, None, None]), 0.0)
    q_decay = jnp.exp((idx + 1.0)[None, :] * log_gamma[:, None])
    k_decay = jnp.exp((C - 1.0 - idx)[None, :] * log_gamma[:, None])
    chunk_decay = jnp.exp(C * log_gamma)
    return decay_in, q_decay[..., None], k_decay[..., None], chunk_decay


def kernel(x, positions, l0_attn_norm, l0_w_in, l0_w_out, l0_sinks, l0_ffn_norm, l0_w_a, l0_w_b, l0_conv_w, l0_conv_b, l0_w_down, l1_attn_norm, l1_w_in, l1_w_out, l1_gn_g, l1_ffn_norm, l1_w_a, l1_w_b, l1_conv_w, l1_conv_b, l1_w_down, l2_attn_norm, l2_w_in, l2_w_out, l2_ffn_norm, l2_w_a, l2_w_b, l2_conv_w, l2_conv_b, l2_w_down, l3_attn_norm, l3_w_in, l3_w_out, l3_sinks, l3_ffn_norm, l3_w_a, l3_w_b, l3_conv_w, l3_conv_b, l3_w_down, final_norm):
    mixers = [
        (l0_attn_norm, l0_w_in, l0_w_out, l0_sinks),
        (l1_attn_norm, l1_w_in, l1_w_out, l1_gn_g),
        (l2_attn_norm, l2_w_in, l2_w_out, None),
        (l3_attn_norm, l3_w_in, l3_w_out, l3_sinks),
    ]
    ffns = [
        (l0_ffn_norm, l0_w_a, l0_w_b, l0_conv_w, l0_conv_b, l0_w_down),
        (l1_ffn_norm, l1_w_a, l1_w_b, l1_conv_w, l1_conv_b, l1_w_down),
        (l2_ffn_norm, l2_w_a, l2_w_b, l2_conv_w, l2_conv_b, l2_w_down),
        (l3_ffn_norm, l3_w_a, l3_w_b, l3_conv_w, l3_conv_b, l3_w_down),
    ]
    cosf, sins = _head_rope_tables(positions)
    ang_r = _rope_angles(positions, RET_KEY_DIM).reshape(TOKENS, RET_KEY_DIM // 2)
    cos_r, sin_r = jnp.cos(ang_r), jnp.sin(ang_r)
    decay_in, q_decay, k_decay, chunk_decay = _retention_decays()

    x2d = x.reshape(TOKENS, D_MODEL)
    for i in range(DEPTH):
        norm_g, w_in, w_out, extra = mixers[i]
        m = i % N_MIXERS
        w_in_b = w_in.astype(bf16)
        if m == 1:
            proj = _ret_proj(x2d, norm_g, w_in_b, cos_r, sin_r)
            y = _ret_core(proj.reshape(BATCH, SEQ, RET_IN_WIDTH), extra, decay_in, q_decay, k_decay, chunk_decay)
            y2d = y.reshape(TOKENS, RET_HEADS * RET_VAL_DIM)
        else:
            q, k, v = _attn_proj(x2d.reshape(BATCH, SEQ, D_MODEL), norm_g, w_in_b, cosf, sins)
            y = _swa(q, k, v, extra) if m == 0 else _moba(q, k, v)
            y2d = y.reshape(TOKENS, D_MODEL)
        fg, wa, wb, cw, cb, wd = ffns[i]
        x2d = _ffn(y2d, w_out.astype(bf16), x2d, fg, wa.astype(bf16), wb.astype(bf16), cw, cb, wd.astype(bf16),
                   final_norm if i == DEPTH - 1 else None)
    return x2d.reshape(BATCH, SEQ, D_MODEL)
```

```python
import functools

import jax
import jax.numpy as jnp
from jax import lax
from jax.experimental import pallas as pl
from jax.experimental.pallas import tpu as pltpu

D_MODEL = 1024
BATCH = 4
SEQ = 4096
DEPTH = 4
N_MIXERS = 3
HEAD_DIM = 64
N_Q_HEADS = D_MODEL // HEAD_DIM
N_KV_HEADS = 4
GQA_GROUP = N_Q_HEADS // N_KV_HEADS
QKV_WIDTH = (N_Q_HEADS + 2 * N_KV_HEADS) * HEAD_DIM
ROPE_THETA = 10000.0
SWA_WINDOW = 128
RET_HEADS = 4
RET_KEY_DIM = D_MODEL // RET_HEADS
RET_VAL_DIM = 2 * RET_KEY_DIM
RET_IN_WIDTH = 2 * D_MODEL + 2 * RET_HEADS * RET_VAL_DIM
RET_CHUNK = 128
MOBA_BLOCK = 256
MOBA_TOPK = 3
D_FF = ((8 * D_MODEL // 3 + 127) // 128) * 128
CONV_WIDTH = 3
EPS = 1e-6
NEG_INF = -1e30

LANES = 128
SUBLANES = 8
MXU_WIDTH = 256
VMEM_LIMIT_BYTES = 56 * 1024 * 1024

TOKENS = BATCH * SEQ
PROJ_TM = 512
ATT_TQ = 1024
RET_TQ = 512
MOBA_ROW_CHUNK = 128
MOBA_GROUPS = 4
FFN_TM = 512
FFN_CHUNK = 256
FFN_HALO = SUBLANES

f32 = jnp.float32
bf16 = jnp.bfloat16
NT_DIMS = (((1,), (1,)), ((), ()))


def _params(*semantics):
    return pltpu.CompilerParams(dimension_semantics=semantics, vmem_limit_bytes=VMEM_LIMIT_BYTES)


def _rms_norm(x, gain):
    ms = jnp.mean(x * x, axis=-1, keepdims=True)
    return x * lax.rsqrt(ms + EPS) * gain


def _silu(x):
    half = 0.5 * x
    return half + half * jnp.tanh(half)


def _normalize_pair(pv_even, pv_odd, extra_even, extra_odd):
    low = lax.broadcasted_iota(jnp.int32, pv_even.shape, 1) < HEAD_DIM
    straight = jnp.where(low, pv_even, pv_odd)
    crossed = pltpu.roll(jnp.where(low, pv_odd, pv_even), HEAD_DIM, 1)
    num = jnp.where(low, straight, crossed)
    den = jnp.where(low, crossed + extra_even, straight + extra_odd)
    return num / den


def _attn_proj_kernel(x_ref, g_ref, w_ref, cos_ref, sin_ref, q_ref, k_ref, v_ref):
    t = pl.program_id(1)
    h = _rms_norm(x_ref[...], g_ref[...]).astype(bf16)
    cosf = cos_ref[...]
    sins = sin_ref[...]
    lane = lax.broadcasted_iota(jnp.int32, (PROJ_TM, LANES), 1)
    first_half = (lane % HEAD_DIM) < (HEAD_DIM // 2)
    low = lane < HEAD_DIM
    blk = (t * PROJ_TM + lax.broadcasted_iota(jnp.int32, (PROJ_TM, LANES), 0)) // MOBA_BLOCK
    k_tail = jnp.where(lane == HEAD_DIM + blk, 1.0, 0.0)
    v_tail = 1.0

    def split(r, tail):
        return (jnp.where(low, r, tail).astype(bf16),
                jnp.where(low, pltpu.roll(r, HEAD_DIM, 1), tail).astype(bf16))

    def rope(xc):
        partner = jnp.where(first_half, pltpu.roll(xc, LANES - HEAD_DIM // 2, 1), pltpu.roll(xc, HEAD_DIM // 2, 1))
        return xc * cosf + partner * sins

    n_q = N_Q_HEADS * HEAD_DIM // LANES
    n_k = N_KV_HEADS * HEAD_DIM // LANES
    groups_per_dot = MXU_WIDTH // LANES
    for d in range(QKV_WIDTH // MXU_WIDTH):
        proj = jnp.dot(h, w_ref[:, d * MXU_WIDTH:(d + 1) * MXU_WIDTH], preferred_element_type=f32)
        for e in range(groups_per_dot):
            c = d * groups_per_dot + e
            xc = proj[:, e * LANES:(e + 1) * LANES]
            if c < n_q:
                q_ref[2 * c], q_ref[2 * c + 1] = split(rope(xc) * (HEAD_DIM ** -0.5), 0.0)
            elif c < n_q + n_k:
                k_ref[2 * (c - n_q)], k_ref[2 * (c - n_q) + 1] = split(rope(xc), k_tail)
            else:
                v_ref[2 * (c - n_q - n_k)], v_ref[2 * (c - n_q - n_k) + 1] = split(xc, v_tail)


def _attn_proj(x, gain, w_bf16, cosf, sins):
    nt = SEQ // PROJ_TM
    return pl.pallas_call(
        _attn_proj_kernel,
        grid=(BATCH, nt),
        in_specs=[
            pl.BlockSpec((None, PROJ_TM, D_MODEL), lambda b, t: (b, t, 0)),
            pl.BlockSpec((1, D_MODEL), lambda b, t: (0, 0)),
            pl.BlockSpec((D_MODEL, QKV_WIDTH), lambda b, t: (0, 0)),
            pl.BlockSpec((None, PROJ_TM, LANES), lambda b, t: (b, t, 0)),
            pl.BlockSpec((None, PROJ_TM, LANES), lambda b, t: (b, t, 0)),
        ],
        out_specs=[
            pl.BlockSpec((None, N_Q_HEADS, PROJ_TM, LANES), lambda b, t: (b, 0, t, 0)),
            pl.BlockSpec((None, N_KV_HEADS, PROJ_TM, LANES), lambda b, t: (b, 0, t, 0)),
            pl.BlockSpec((None, N_KV_HEADS, PROJ_TM, LANES), lambda b, t: (b, 0, t, 0)),
        ],
        out_shape=[
            jax.ShapeDtypeStruct((BATCH, N_Q_HEADS, SEQ, LANES), bf16),
            jax.ShapeDtypeStruct((BATCH, N_KV_HEADS, SEQ, LANES), bf16),
            jax.ShapeDtypeStruct((BATCH, N_KV_HEADS, SEQ, LANES), bf16),
        ],
        compiler_params=_params("parallel", "parallel"),
        name="attn_proj",
    )(x, gain.reshape(1, D_MODEL), w_bf16, cosf, sins)


def _swa_kernel(sinks_ref, q_ref, k_ref, v_ref, o_ref, s_ref):
    g = pl.program_id(1)
    t = pl.program_id(2)
    W = SWA_WINDOW
    nsub = ATT_TQ // W
    qi = lax.broadcasted_iota(jnp.int32, (W, 2 * W), 0)
    col = lax.broadcasted_iota(jnp.int32, (W, 2 * W), 1)
    band = jnp.where((col > qi) & (col <= qi + W), 0.0, NEG_INF)
    first = jnp.where(col <= qi, 0.0, NEG_INF)

    def key_start(c):
        return pl.multiple_of(jnp.maximum(t * ATT_TQ + (c - 1) * W, 0), W)

    def scores(c):
        q4 = q_ref[:, c * W:(c + 1) * W, :].reshape(GQA_GROUP * W, LANES)
        kk = k_ref[pl.ds(key_start(c), 2 * W), :]
        s_ref[c] = lax.dot_general(q4, kk, NT_DIMS, preferred_element_type=f32)

    def head(c, j, vv, bias):
        sink = sinks_ref[g * GQA_GROUP + j]
        s = s_ref[c, j * W:(j + 1) * W, :] + bias
        s0, s1 = s[:, :LANES], s[:, LANES:]
        m = jnp.maximum(jnp.max(jnp.maximum(s0, s1), axis=1, keepdims=True), sink)
        p = jnp.concatenate([jnp.exp(s0 - m), jnp.exp(s1 - m)], axis=1).astype(bf16)
        return jnp.dot(p, vv, preferred_element_type=f32), jnp.exp(sink - m)

    scores(0)
    for c in range(nsub):
        if c + 1 < nsub:
            scores(c + 1)
        vv = v_ref[pl.ds(key_start(c), 2 * W), :]
        bias = jnp.where(t == 0, first, band) if c == 0 else band
        for jj in range(GQA_GROUP // 2):
            pv0, e0 = head(c, 2 * jj, vv, bias)
            pv1, e1 = head(c, 2 * jj + 1, vv, bias)
            o_ref[c * W:(c + 1) * W, jj * LANES:(jj + 1) * LANES] = _normalize_pair(pv0, pv1, e0, e1).astype(bf16)


def _swa(q, k, v, sinks):
    nt = SEQ // ATT_TQ
    gw = GQA_GROUP * HEAD_DIM
    return pl.pallas_call(
        _swa_kernel,
        grid=(BATCH, N_KV_HEADS, nt),
        in_specs=[
            pl.BlockSpec(memory_space=pltpu.SMEM),
            pl.BlockSpec((None, GQA_GROUP, ATT_TQ, LANES), lambda b, g, t: (b, g, t, 0)),
            pl.BlockSpec((None, None, SEQ, LANES), lambda b, g, t: (b, g, 0, 0)),
            pl.BlockSpec((None, None, SEQ, LANES), lambda b, g, t: (b, g, 0, 0)),
        ],
        out_specs=pl.BlockSpec((None, ATT_TQ, gw), lambda b, g, t: (b, t, g)),
        out_shape=jax.ShapeDtypeStruct((BATCH, SEQ, D_MODEL), bf16),
        scratch_shapes=[pltpu.VMEM((ATT_TQ // SWA_WINDOW, GQA_GROUP * SWA_WINDOW, 2 * SWA_WINDOW), f32)],
        compiler_params=_params("parallel", "parallel", "parallel"),
        name="swa_core",
    )(sinks, q, k, v)


def _moba_kernel(q_ref, k_ref, v_ref, o_ref, kmean_ref, qa_ref, sa_ref, sb_ref, m_ref, acc_ref):
    i = pl.program_id(2)
    BS = MOBA_BLOCK
    nb = SEQ // BS
    G = MOBA_GROUPS
    grows = GQA_GROUP * BS
    rows = G * grows

    @pl.when(i == 0)
    def _():
        lane = lax.broadcasted_iota(jnp.int32, (1, LANES), 1)
        for g in range(G):
            for n in range(nb):
                kb = k_ref[g, n * BS:(n + 1) * BS, :].astype(f32)
                kmean_ref[g, n:n + 1, :] = jnp.where(lane < HEAD_DIM, jnp.mean(kb, axis=0, keepdims=True), 0.0)

    q = q_ref[...].reshape(rows, LANES)
    gate_t = jnp.concatenate(
        [lax.dot_general(kmean_ref[g].astype(bf16), q[g * grows:(g + 1) * grows], NT_DIMS,
                         preferred_element_type=f32) for g in range(G)], axis=1)
    blk = lax.broadcasted_iota(jnp.int32, (nb, rows), 0)
    past = blk < i
    gm = jnp.where(past, gate_t, NEG_INF)
    sel = jnp.zeros((nb, rows), jnp.int32)
    for _ in range(MOBA_TOPK):
        mx = jnp.max(gm, axis=0, keepdims=True)
        cand = (gm == mx) & past & (sel == 0)
        idx = jnp.min(jnp.where(cand, blk, nb), axis=0, keepdims=True)
        pick = blk == idx
        sel = jnp.where(pick, 1, sel)
        gm = jnp.where(pick, NEG_INF, gm)
    bias_t = jnp.where((sel == 1) | jnp.logical_not(past), 0.0, NEG_INF)
    bias_pad_t = jnp.concatenate(
        [jnp.zeros((HEAD_DIM, rows), f32), bias_t, jnp.zeros((LANES - HEAD_DIM - nb, rows), f32)], axis=0)
    qa_ref[...] = (q.astype(f32) + bias_pad_t.T).astype(bf16)

    RC = MOBA_ROW_CHUNK
    qi = lax.broadcasted_iota(jnp.int32, (RC, LANES), 0)
    kj = lax.broadcasted_iota(jnp.int32, (RC, LANES), 1)

    def block_start(n):
        return pl.multiple_of(jnp.minimum(n, nb - 1) * BS, BS)

    def plain_rows(g):
        return q_ref[g * GQA_GROUP:(g + 1) * GQA_GROUP].reshape(grows, LANES)

    def biased_rows(g):
        return qa_ref[g * grows:(g + 1) * grows, :]

    def scores(g, n, s_ref, lhs_rows=biased_rows):
        k_n = k_ref[g, pl.ds(block_start(n), BS), :]
        s_ref[g * grows:(g + 1) * grows, :] = lax.dot_general(lhs_rows(g), k_n, NT_DIMS, preferred_element_type=f32)

    def softmax_pv(g, n, s_ref, own):
        v_n = v_ref[g, pl.ds(block_start(n), BS), :]
        for c in range(grows // RC):
            rs = slice(g * grows + c * RC, g * grows + (c + 1) * RC)
            s = s_ref[rs, :]
            s0, s1 = s[:, :LANES], s[:, LANES:]
            if own:
                q0 = (c * RC) % BS
                s0 = jnp.where(kj <= qi + q0, s0, NEG_INF)
                s1 = jnp.where(kj + LANES <= qi + q0, s1, NEG_INF)
            row_max = jnp.max(jnp.maximum(s0, s1), axis=1, keepdims=True)
            if own:
                m_new = jnp.broadcast_to(row_max, (RC, LANES))
            else:
                m_old = m_ref[rs, :]
                m_new = jnp.maximum(m_old, row_max)
            p = jnp.concatenate([jnp.exp(s0 - m_new), jnp.exp(s1 - m_new)], axis=1).astype(bf16)
            pv = jnp.dot(p, v_n, preferred_element_type=f32)
            if own:
                acc_ref[rs, :] = pv
            else:
                acc_ref[rs, :] = jnp.exp(m_old - m_new) * acc_ref[rs, :] + pv
            m_ref[rs, :] = m_new

    def step(n_next, next_ref, n_cur, cur_ref, own):
        for g in range(G):
            scores(g, n_next, next_ref)
            softmax_pv(g, n_cur, cur_ref, own)

    for g in range(G):
        scores(g, i, sa_ref, plain_rows)
    step(0, sb_ref, i, sa_ref, True)

    def body(n, carry):
        @pl.when(n % 2 == 0)
        def _():
            step(n + 1, sa_ref, n, sb_ref, False)

        @pl.when(n % 2 == 1)
        def _():
            step(n + 1, sb_ref, n, sa_ref, False)

        return carry

    lax.fori_loop(0, i, body, 0)
    for jj in range(G * GQA_GROUP // 2):
        a0 = acc_ref[2 * jj * BS:(2 * jj + 1) * BS, :]
        a1 = acc_ref[(2 * jj + 1) * BS:(2 * jj + 2) * BS, :]
        o_ref[:, jj * LANES:(jj + 1) * LANES] = _normalize_pair(a0, a1, 0.0, 0.0).astype(bf16)


def _moba(q, k, v):
    nb = SEQ // MOBA_BLOCK
    G = MOBA_GROUPS
    gw = G * GQA_GROUP * HEAD_DIM
    rows = G * GQA_GROUP * MOBA_BLOCK
    return pl.pallas_call(
        _moba_kernel,
        grid=(BATCH, N_KV_HEADS // G, nb),
        in_specs=[
            pl.BlockSpec((None, G * GQA_GROUP, MOBA_BLOCK, LANES), lambda b, g, i: (b, g, i, 0)),
            pl.BlockSpec((None, G, SEQ, LANES), lambda b, g, i: (b, g, 0, 0)),
            pl.BlockSpec((None, G, SEQ, LANES), lambda b, g, i: (b, g, 0, 0)),
        ],
        out_specs=pl.BlockSpec((None, MOBA_BLOCK, gw), lambda b, g, i: (b, i, g)),
        out_shape=jax.ShapeDtypeStruct((BATCH, SEQ, D_MODEL), bf16),
        scratch_shapes=[
            pltpu.VMEM((G, nb, LANES), f32),
            pltpu.VMEM((rows, LANES), bf16),
            pltpu.VMEM((rows, MOBA_BLOCK), f32),
            pltpu.VMEM((rows, MOBA_BLOCK), f32),
            pltpu.VMEM((rows, LANES), f32),
            pltpu.VMEM((rows, LANES), f32),
        ],
        compiler_params=_params("parallel", "parallel", "arbitrary"),
        name="moba_core",
    )(q, k, v)


def _ret_proj_kernel(x_ref, g_ref, w_ref, cos_ref, sin_ref, o_ref):
    h = _rms_norm(x_ref[...], g_ref[...]).astype(bf16)
    cos = cos_ref[...]
    sin = sin_ref[...]
    half = RET_KEY_DIM // 2
    n_rope = 2 * RET_HEADS
    for c in range(RET_IN_WIDTH // RET_KEY_DIM):
        c0 = c * RET_KEY_DIM
        proj = jnp.dot(h, w_ref[:, c0:c0 + RET_KEY_DIM], preferred_element_type=f32)
        if c < n_rope:
            x1, x2 = proj[:, :half], proj[:, half:]
            r1, r2 = x1 * cos - x2 * sin, x2 * cos + x1 * sin
            if c >= RET_HEADS:
                r1, r2 = r1 * (RET_KEY_DIM ** -0.5), r2 * (RET_KEY_DIM ** -0.5)
            o_ref[:, c0:c0 + half] = r1.astype(bf16)
            o_ref[:, c0 + half:c0 + RET_KEY_DIM] = r2.astype(bf16)
        else:
            o_ref[:, c0:c0 + RET_KEY_DIM] = proj.astype(bf16)


def _ret_proj(x2d, gain, w_bf16, cos, sin):
    nt = TOKENS // PROJ_TM
    return pl.pallas_call(
        _ret_proj_kernel,
        grid=(nt,),
        in_specs=[
            pl.BlockSpec((PROJ_TM, D_MODEL), lambda t: (t, 0)),
            pl.BlockSpec((1, D_MODEL), lambda t: (0, 0)),
            pl.BlockSpec((D_MODEL, RET_IN_WIDTH), lambda t: (0, 0), pipeline_mode=pl.Buffered(1)),
            pl.BlockSpec((PROJ_TM, LANES), lambda t: (t, 0)),
            pl.BlockSpec((PROJ_TM, LANES), lambda t: (t, 0)),
        ],
        out_specs=pl.BlockSpec((PROJ_TM, RET_IN_WIDTH), lambda t: (t, 0)),
        out_shape=jax.ShapeDtypeStruct((TOKENS, RET_IN_WIDTH), bf16),
        compiler_params=_params("parallel"),
        name="ret_proj",
    )(x2d, gain.reshape(1, D_MODEL), w_bf16, cos, sin)


def _ret_core_kernel(cd_ref, q_ref, k_ref, v_ref, g_ref, din_ref, qdec_ref, kdec_ref, gn_ref, y_ref, r_ref):
    t = pl.program_id(1)
    dk, dv = RET_KEY_DIM, RET_VAL_DIM

    @pl.when(t == 0)
    def _():
        r_ref[...] = jnp.zeros_like(r_ref)

    for h in range(RET_HEADS):
        qn = q_ref[:, h * dk:(h + 1) * dk]
        kn = k_ref[:, h * dk:(h + 1) * dk]
        vn = v_ref[:, h * dv:(h + 1) * dv]
        sc = lax.dot_general(qn, kn, NT_DIMS, preferred_element_type=f32) * din_ref[h]
        r_old = r_ref[h]
        qd = (qn.astype(f32) * qdec_ref[h]).astype(bf16)
        o = jnp.dot(jnp.concatenate([sc.astype(bf16), qd], axis=1),
                    jnp.concatenate([vn, r_old.astype(bf16)], axis=0), preferred_element_type=f32)
        kdt = (kn.astype(f32) * kdec_ref[h]).T.astype(bf16)
        r_ref[h] = cd_ref[h] * r_old + jnp.dot(kdt, vn, preferred_element_type=f32)
        mu = jnp.mean(o, axis=-1, keepdims=True)
        d = o - mu
        var = jnp.mean(d * d, axis=-1, keepdims=True)
        on = d * lax.rsqrt(var + EPS) * gn_ref[:, h * dv:(h + 1) * dv]
        y_ref[:, h * dv:(h + 1) * dv] = (_silu(g_ref[:, h * dv:(h + 1) * dv].astype(f32)) * on).astype(bf16)


def _ret_core(proj3d, gn_g, decay_in, q_decay, k_decay, chunk_decay):
    nt = SEQ // RET_TQ
    H, dk, dv, C = RET_HEADS, RET_KEY_DIM, RET_VAL_DIM, RET_TQ
    assert 2 * H * dk == H * dv
    resident = dict(pipeline_mode=pl.Buffered(1))
    return pl.pallas_call(
        _ret_core_kernel,
        grid=(BATCH, nt),
        in_specs=[
            pl.BlockSpec(memory_space=pltpu.SMEM),
            pl.BlockSpec((None, RET_TQ, H * dk), lambda b, t: (b, t, 0)),
            pl.BlockSpec((None, RET_TQ, H * dk), lambda b, t: (b, t, 1)),
            pl.BlockSpec((None, RET_TQ, H * dv), lambda b, t: (b, t, 1)),
            pl.BlockSpec((None, RET_TQ, H * dv), lambda b, t: (b, t, 2)),
            pl.BlockSpec((H, C, C), lambda b, t: (0, 0, 0), **resident),
            pl.BlockSpec((H, C, 1), lambda b, t: (0, 0, 0)),
            pl.BlockSpec((H, C, 1), lambda b, t: (0, 0, 0)),
            pl.BlockSpec((1, H * dv), lambda b, t: (0, 0)),
        ],
        out_specs=pl.BlockSpec((None, RET_TQ, H * dv), lambda b, t: (b, t, 0)),
        out_shape=jax.ShapeDtypeStruct((BATCH, SEQ, H * dv), bf16),
        scratch_shapes=[pltpu.VMEM((H, dk, dv), f32)],
        compiler_params=_params("parallel", "arbitrary"),
        name="ret_core",
    )(chunk_decay, proj3d, proj3d, proj3d, proj3d, decay_in, q_decay, k_decay, gn_g.reshape(1, H * dv))


def _ffn_kernel(y_ref, wo_ref, x_ref, g_ref, wa_ref, wb_ref, cw_ref, cb_ref, wd_ref, fg_ref, o_ref,
                aprev_ref, h_ref, hid_ref, *, final_norm):
    t = pl.program_id(0)

    @pl.when(t == 0)
    def _():
        aprev_ref[...] = jnp.zeros_like(aprev_ref)

    x1 = x_ref[...] + jnp.dot(y_ref[...], wo_ref[...], preferred_element_type=f32)
    o_ref[...] = x1
    h_ref[...] = _rms_norm(x1, g_ref[...]).astype(bf16)
    seq_start = (t * FFN_TM) % SEQ == 0
    for c0 in range(0, D_FF, FFN_CHUNK):
        cs = slice(c0, min(c0 + FFN_CHUNK, D_FF))
        a = jnp.dot(h_ref[...], wa_ref[:, cs], preferred_element_type=f32)
        b = jnp.dot(h_ref[...], wb_ref[:, cs], preferred_element_type=f32)
        prev = jnp.where(seq_start, 0.0, aprev_ref[:, cs])
        aprev_ref[:, cs] = a[FFN_TM - FFN_HALO:, :]
        a_full = jnp.concatenate([prev, a], axis=0)
        a1 = pltpu.roll(a_full, 1, 0)[FFN_HALO:]
        a2 = pltpu.roll(a_full, 2, 0)[FFN_HALO:]
        cw = cw_ref[:, cs]
        conv = cw[0:1] * a2 + cw[1:2] * a1 + cw[2:3] * a + cb_ref[:, cs]
        hid_ref[:, cs] = (_silu(conv) * b).astype(bf16)
    out = o_ref[...] + jnp.dot(hid_ref[...], wd_ref[...], preferred_element_type=f32)
    o_ref[...] = _rms_norm(out, fg_ref[...]) if final_norm else out


def _ffn(y2d, w_out, x2d, gain, wa, wb, conv_w, conv_b, wd, final_gain):
    kdim = y2d.shape[1]
    nt = TOKENS // FFN_TM
    resident = dict(pipeline_mode=pl.Buffered(1))
    final_norm = final_gain is not None
    fg = final_gain if final_norm else gain
    return pl.pallas_call(
        functools.partial(_ffn_kernel, final_norm=final_norm),
        grid=(nt,),
        in_specs=[
            pl.BlockSpec((FFN_TM, kdim), lambda t: (t, 0)),
            pl.BlockSpec((kdim, D_MODEL), lambda t: (0, 0), **resident),
            pl.BlockSpec((FFN_TM, D_MODEL), lambda t: (t, 0)),
            pl.BlockSpec((1, D_MODEL), lambda t: (0, 0)),
            pl.BlockSpec((D_MODEL, D_FF), lambda t: (0, 0), **resident),
            pl.BlockSpec((D_MODEL, D_FF), lambda t: (0, 0), **resident),
            pl.BlockSpec((CONV_WIDTH, D_FF), lambda t: (0, 0)),
            pl.BlockSpec((1, D_FF), lambda t: (0, 0)),
            pl.BlockSpec((D_FF, D_MODEL), lambda t: (0, 0), **resident),
            pl.BlockSpec((1, D_MODEL), lambda t: (0, 0)),
        ],
        out_specs=pl.BlockSpec((FFN_TM, D_MODEL), lambda t: (t, 0)),
        out_shape=jax.ShapeDtypeStruct((TOKENS, D_MODEL), f32),
        scratch_shapes=[
            pltpu.VMEM((FFN_HALO, D_FF), f32),
            pltpu.VMEM((FFN_TM, D_MODEL), bf16),
            pltpu.VMEM((FFN_TM, D_FF), bf16),
        ],
        compiler_params=_params("arbitrary"),
        name="out_proj_conv_ffn",
    )(y2d, w_out, x2d, gain.reshape(1, D_MODEL), wa, wb, conv_w, conv_b.reshape(1, D_FF), wd, fg.reshape(1, D_MODEL))


def _rope_angles(positions, dim):
    inv = 1.0 / (ROPE_THETA ** (jnp.arange(0, dim, 2, dtype=f32) / dim))
    return positions.astype(f32)[..., None] * inv


def _head_rope_tables(positions):
    ang = _rope_angles(positions, HEAD_DIM)
    cos, sin = jnp.cos(ang), jnp.sin(ang)
    reps = LANES // HEAD_DIM
    cosf = jnp.tile(jnp.concatenate([cos, cos], axis=-1), (1, 1, reps))
    sins = jnp.tile(jnp.concatenate([-sin, sin], axis=-1), (1, 1, reps))
    return cosf, sins


def _retention_decays():
    H, C = RET_HEADS, RET_TQ
    log_gamma = jnp.log(1.0 - 2.0 ** (-5.0 - jnp.arange(H, dtype=f32)))
    idx = jnp.arange(C, dtype=f32)
    diff = idx[:, None] - idx[None, :]
    decay_in = jnp.where(diff[None] >= 0, jnp.exp(jnp.maximum(diff, 0.0)[None] * log_gamma[:, None, None]), 0.0)
    q_decay = jnp.exp((idx + 1.0)[None, :] * log_gamma[:, None])
    k_decay = jnp.exp((C - 1.0 - idx)[None, :] * log_gamma[:, None])
    chunk_decay = jnp.exp(C * log_gamma)
    return decay_in, q_decay[..., None], k_decay[..., None], chunk_decay


def kernel(x, positions, l0_attn_norm, l0_w_in, l0_w_out, l0_sinks, l0_ffn_norm, l0_w_a, l0_w_b, l0_conv_w, l0_conv_b, l0_w_down, l1_attn_norm, l1_w_in, l1_w_out, l1_gn_g, l1_ffn_norm, l1_w_a, l1_w_b, l1_conv_w, l1_conv_b, l1_w_down, l2_attn_norm, l2_w_in, l2_w_out, l2_ffn_norm, l2_w_a, l2_w_b, l2_conv_w, l2_conv_b, l2_w_down, l3_attn_norm, l3_w_in, l3_w_out, l3_sinks, l3_ffn_norm, l3_w_a, l3_w_b, l3_conv_w, l3_conv_b, l3_w_down, final_norm):
    mixers = [
        (l0_attn_norm, l0_w_in, l0_w_out, l0_sinks),
        (l1_attn_norm, l1_w_in, l1_w_out, l1_gn_g),
        (l2_attn_norm, l2_w_in, l2_w_out, None),
        (l3_attn_norm, l3_w_in, l3_w_out, l3_sinks),
    ]
    ffns = [
        (l0_ffn_norm, l0_w_a, l0_w_b, l0_conv_w, l0_conv_b, l0_w_down),
        (l1_ffn_norm, l1_w_a, l1_w_b, l1_conv_w, l1_conv_b, l1_w_down),
        (l2_ffn_norm, l2_w_a, l2_w_b, l2_conv_w, l2_conv_b, l2_w_down),
        (l3_ffn_norm, l3_w_a, l3_w_b, l3_conv_w, l3_conv_b, l3_w_down),
    ]
    cosf, sins = _head_rope_tables(positions)
    ang_r = _rope_angles(positions, RET_KEY_DIM).reshape(TOKENS, RET_KEY_DIM // 2)
    cos_r, sin_r = jnp.cos(ang_r), jnp.sin(ang_r)
    decay_in, q_decay, k_decay, chunk_decay = _retention_decays()

    x2d = x.reshape(TOKENS, D_MODEL)
    for i in range(DEPTH):
        norm_g, w_in, w_out, extra = mixers[i]
        m = i % N_MIXERS
        w_in_b = w_in.astype(bf16)
        if m == 1:
            proj = _ret_proj(x2d, norm_g, w_in_b, cos_r, sin_r)
            y = _ret_core(proj.reshape(BATCH, SEQ, RET_IN_WIDTH), extra, decay_in, q_decay, k_decay, chunk_decay)
            y2d = y.reshape(TOKENS, RET_HEADS * RET_VAL_DIM)
        else:
            q, k, v = _attn_proj(x2d.reshape(BATCH, SEQ, D_MODEL), norm_g, w_in_b, cosf, sins)
            y = _swa(q, k, v, extra) if m == 0 else _moba(q, k, v)
            y2d = y.reshape(TOKENS, D_MODEL)
        fg, wa, wb, cw, cb, wd = ffns[i]
        x2d = _ffn(y2d, w_out.astype(bf16), x2d, fg, wa.astype(bf16), wb.astype(bf16), cw, cb, wd.astype(bf16),
                   final_norm if i == DEPTH - 1 else None)
    return x2d.reshape(BATCH, SEQ, D_MODEL)
```

```python
import functools

import jax
import jax.numpy as jnp
from jax import lax
from jax.experimental import pallas as pl
from jax.experimental.pallas import tpu as pltpu

D_MODEL = 1024
BATCH = 4
SEQ = 4096
DEPTH = 4
N_MIXERS = 3
HEAD_DIM = 64
N_Q_HEADS = D_MODEL // HEAD_DIM
N_KV_HEADS = 4
GQA_GROUP = N_Q_HEADS // N_KV_HEADS
QKV_WIDTH = (N_Q_HEADS + 2 * N_KV_HEADS) * HEAD_DIM
ROPE_THETA = 10000.0
SWA_WINDOW = 128
RET_HEADS = 4
RET_KEY_DIM = D_MODEL // RET_HEADS
RET_VAL_DIM = 2 * RET_KEY_DIM
RET_IN_WIDTH = 2 * D_MODEL + 2 * RET_HEADS * RET_VAL_DIM
RET_CHUNK = 128
MOBA_BLOCK = 256
MOBA_TOPK = 3
D_FF = ((8 * D_MODEL // 3 + 127) // 128) * 128
CONV_WIDTH = 3
EPS = 1e-6
NEG_INF = -1e30

LANES = 128
SUBLANES = 8
MXU_WIDTH = 256
VMEM_LIMIT_BYTES = 56 * 1024 * 1024

TOKENS = BATCH * SEQ
PROJ_TM = 512
ATT_TQ = 1024
RET_TQ = 512
MOBA_ROW_CHUNK = 128
MOBA_GROUPS = 4
FFN_TM = 512
FFN_CHUNK = 256
FFN_HALO = SUBLANES

f32 = jnp.float32
bf16 = jnp.bfloat16
NT_DIMS = (((1,), (1,)), ((), ()))


def _params(*semantics):
    return pltpu.CompilerParams(dimension_semantics=semantics, vmem_limit_bytes=VMEM_LIMIT_BYTES)


def _rms_norm(x, gain):
    ms = jnp.mean(x * x, axis=-1, keepdims=True)
    return x * lax.rsqrt(ms + EPS) * gain


def _silu(x):
    half = 0.5 * x
    return half + half * jnp.tanh(half)


def _normalize_pair(pv_even, pv_odd, extra_even, extra_odd):
    low = lax.broadcasted_iota(jnp.int32, pv_even.shape, 1) < HEAD_DIM
    straight = jnp.where(low, pv_even, pv_odd)
    crossed = pltpu.roll(jnp.where(low, pv_odd, pv_even), HEAD_DIM, 1)
    num = jnp.where(low, straight, crossed)
    den = jnp.where(low, crossed + extra_even, straight + extra_odd)
    return num / den


QK_HALF = HEAD_DIM // 2
def _attn_proj_kernel(x_ref, g_ref, w_ref, cos_ref, sin_ref, q_ref, k_ref, v_ref):
    t = pl.program_id(1)
    h = _rms_norm(x_ref[...], g_ref[...]).astype(bf16)
    cosf = cos_ref[...]
    sins = sin_ref[...]
    lane = lax.broadcasted_iota(jnp.int32, (PROJ_TM, LANES), 1)
    low = lane < HEAD_DIM
    qk_lanes = (lane % HEAD_DIM) < QK_HALF
    blk = (t * PROJ_TM + lax.broadcasted_iota(jnp.int32, (PROJ_TM, LANES), 0)) // MOBA_BLOCK
    k_tail = jnp.where(lane == QK_HALF + blk, 1.0, 0.0)

    def split_v(r):
        return jnp.where(low, r, 1.0).astype(bf16), jnp.where(low, pltpu.roll(r, HEAD_DIM, 1), 1.0).astype(bf16)

    def split_qk(r, tail):
        return (jnp.where(qk_lanes, r, tail).astype(bf16),
                jnp.where(qk_lanes, pltpu.roll(r, LANES - QK_HALF, 1), tail).astype(bf16))

    def rope(xc):
        return xc * cosf + pltpu.roll(xc, HEAD_DIM, 1) * sins

    n_q = N_Q_HEADS * HEAD_DIM // LANES
    n_k = N_KV_HEADS * HEAD_DIM // LANES
    groups_per_dot = MXU_WIDTH // LANES
    for d in range(QKV_WIDTH // MXU_WIDTH):
        proj = jnp.dot(h, w_ref[:, d * MXU_WIDTH:(d + 1) * MXU_WIDTH], preferred_element_type=f32)
        for e in range(groups_per_dot):
            c = d * groups_per_dot + e
            xc = proj[:, e * LANES:(e + 1) * LANES]
            if c < n_q:
                q_ref[2 * c], q_ref[2 * c + 1] = split_qk(rope(xc) * (HEAD_DIM ** -0.5), 0.0)
            elif c < n_q + n_k:
                k_ref[2 * (c - n_q)], k_ref[2 * (c - n_q) + 1] = split_qk(rope(xc), k_tail)
            else:
                v_ref[2 * (c - n_q - n_k)], v_ref[2 * (c - n_q - n_k) + 1] = split_v(xc)


def _attn_proj(x, gain, w_bf16, cosf, sins):
    nt = SEQ // PROJ_TM
    return pl.pallas_call(
        _attn_proj_kernel,
        grid=(BATCH, nt),
        in_specs=[
            pl.BlockSpec((None, PROJ_TM, D_MODEL), lambda b, t: (b, t, 0)),
            pl.BlockSpec((1, D_MODEL), lambda b, t: (0, 0)),
            pl.BlockSpec((D_MODEL, QKV_WIDTH), lambda b, t: (0, 0)),
            pl.BlockSpec((None, PROJ_TM, LANES), lambda b, t: (b, t, 0)),
            pl.BlockSpec((None, PROJ_TM, LANES), lambda b, t: (b, t, 0)),
        ],
        out_specs=[
            pl.BlockSpec((None, N_Q_HEADS, PROJ_TM, LANES), lambda b, t: (b, 0, t, 0)),
            pl.BlockSpec((None, N_KV_HEADS, PROJ_TM, LANES), lambda b, t: (b, 0, t, 0)),
            pl.BlockSpec((None, N_KV_HEADS, PROJ_TM, LANES), lambda b, t: (b, 0, t, 0)),
        ],
        out_shape=[
            jax.ShapeDtypeStruct((BATCH, N_Q_HEADS, SEQ, LANES), bf16),
            jax.ShapeDtypeStruct((BATCH, N_KV_HEADS, SEQ, LANES), bf16),
            jax.ShapeDtypeStruct((BATCH, N_KV_HEADS, SEQ, LANES), bf16),
        ],
        compiler_params=_params("parallel", "parallel"),
        name="attn_proj",
    )(x, gain.reshape(1, D_MODEL), w_bf16, cosf, sins)


def _swa_kernel(sinks_ref, q_ref, k_ref, v_ref, o_ref, s_ref):
    g = pl.program_id(1)
    t = pl.program_id(2)
    W = SWA_WINDOW
    nsub = ATT_TQ // W
    qi = lax.broadcasted_iota(jnp.int32, (W, 2 * W), 0)
    col = lax.broadcasted_iota(jnp.int32, (W, 2 * W), 1)
    band = jnp.where((col > qi) & (col <= qi + W), 0.0, NEG_INF)
    first = jnp.where(col <= qi, 0.0, NEG_INF)

    def key_start(c):
        return pl.multiple_of(jnp.maximum(t * ATT_TQ + (c - 1) * W, 0), W)

    def scores(c):
        q4 = q_ref[:, c * W:(c + 1) * W, :].reshape(GQA_GROUP * W, LANES)
        kk = k_ref[pl.ds(key_start(c), 2 * W), :]
        s_ref[c] = lax.dot_general(q4, kk, NT_DIMS, preferred_element_type=f32)

    def head(c, j, vv, bias):
        sink = sinks_ref[g * GQA_GROUP + j]
        s = s_ref[c, j * W:(j + 1) * W, :] + bias
        s0, s1 = s[:, :LANES], s[:, LANES:]
        m = jnp.maximum(jnp.max(jnp.maximum(s0, s1), axis=1, keepdims=True), sink)
        p = jnp.concatenate([jnp.exp(s0 - m), jnp.exp(s1 - m)], axis=1).astype(bf16)
        return jnp.dot(p, vv, preferred_element_type=f32), jnp.exp(sink - m)

    scores(0)
    for c in range(nsub):
        if c + 1 < nsub:
            scores(c + 1)
        vv = v_ref[pl.ds(key_start(c), 2 * W), :]
        bias = jnp.where(t == 0, first, band) if c == 0 else band
        for jj in range(GQA_GROUP // 2):
            pv0, e0 = head(c, 2 * jj, vv, bias)
            pv1, e1 = head(c, 2 * jj + 1, vv, bias)
            o_ref[c * W:(c + 1) * W, jj * LANES:(jj + 1) * LANES] = _normalize_pair(pv0, pv1, e0, e1).astype(bf16)


def _swa(q, k, v, sinks):
    nt = SEQ // ATT_TQ
    gw = GQA_GROUP * HEAD_DIM
    return pl.pallas_call(
        _swa_kernel,
        grid=(BATCH, N_KV_HEADS, nt),
        in_specs=[
            pl.BlockSpec(memory_space=pltpu.SMEM),
            pl.BlockSpec((None, GQA_GROUP, ATT_TQ, LANES), lambda b, g, t: (b, g, t, 0)),
            pl.BlockSpec((None, None, SEQ, LANES), lambda b, g, t: (b, g, 0, 0)),
            pl.BlockSpec((None, None, SEQ, LANES), lambda b, g, t: (b, g, 0, 0)),
        ],
        out_specs=pl.BlockSpec((None, ATT_TQ, gw), lambda b, g, t: (b, t, g)),
        out_shape=jax.ShapeDtypeStruct((BATCH, SEQ, D_MODEL), bf16),
        scratch_shapes=[pltpu.VMEM((ATT_TQ // SWA_WINDOW, GQA_GROUP * SWA_WINDOW, 2 * SWA_WINDOW), f32)],
        compiler_params=_params("parallel", "parallel", "parallel"),
        name="swa_core",
    )(sinks, q, k, v)


def _moba_kernel(q_ref, k_ref, v_ref, o_ref, kmean_ref, qa_ref, sa_ref, sb_ref, m_ref, acc_ref):
    i = pl.program_id(2)
    BS = MOBA_BLOCK
    nb = SEQ // BS
    G = MOBA_GROUPS
    grows = GQA_GROUP * BS
    rows = G * grows

    @pl.when(i == 0)
    def _():
        lane = lax.broadcasted_iota(jnp.int32, (1, LANES), 1)
        for g in range(G):
            for n in range(nb):
                kb = k_ref[g, n * BS:(n + 1) * BS, :].astype(f32)
                kmean_ref[g, n:n + 1, :] = jnp.where(lane % HEAD_DIM < QK_HALF, jnp.mean(kb, axis=0, keepdims=True), 0.0)

    q = q_ref[...].reshape(rows, LANES)
    gate_t = jnp.concatenate(
        [lax.dot_general(kmean_ref[g].astype(bf16), q[g * grows:(g + 1) * grows], NT_DIMS,
                         preferred_element_type=f32) for g in range(G)], axis=1)
    blk = lax.broadcasted_iota(jnp.int32, (nb, rows), 0)
    past = blk < i
    gm = jnp.where(past, gate_t, NEG_INF)
    sel = jnp.zeros((nb, rows), jnp.int32)
    for _ in range(MOBA_TOPK):
        mx = jnp.max(gm, axis=0, keepdims=True)
        cand = (gm == mx) & past & (sel == 0)
        idx = jnp.min(jnp.where(cand, blk, nb), axis=0, keepdims=True)
        pick = blk == idx
        sel = jnp.where(pick, 1, sel)
        gm = jnp.where(pick, NEG_INF, gm)
    bias_t = jnp.where((sel == 1) | jnp.logical_not(past), 0.0, NEG_INF)
    bias_pad_t = jnp.concatenate(
        [jnp.zeros((QK_HALF, rows), f32), bias_t, jnp.zeros((LANES - QK_HALF - nb, rows), f32)], axis=0)
    qa_ref[...] = (q.astype(f32) + bias_pad_t.T).astype(bf16)

    RC = MOBA_ROW_CHUNK
    qi = lax.broadcasted_iota(jnp.int32, (RC, LANES), 0)
    kj = lax.broadcasted_iota(jnp.int32, (RC, LANES), 1)

    def block_start(n):
        return pl.multiple_of(jnp.minimum(n, nb - 1) * BS, BS)

    def plain_rows(g):
        return q_ref[g * GQA_GROUP:(g + 1) * GQA_GROUP].reshape(grows, LANES)

    def biased_rows(g):
        return qa_ref[g * grows:(g + 1) * grows, :]

    def scores(g, n, s_ref, lhs_rows=biased_rows):
        k_n = k_ref[g, pl.ds(block_start(n), BS), :]
        s_ref[g * grows:(g + 1) * grows, :] = lax.dot_general(lhs_rows(g), k_n, NT_DIMS, preferred_element_type=f32)

    def softmax_pv(g, n, s_ref, own):
        v_n = v_ref[g, pl.ds(block_start(n), BS), :]
        for c in range(grows // RC):
            rs = slice(g * grows + c * RC, g * grows + (c + 1) * RC)
            s = s_ref[rs, :]
            s0, s1 = s[:, :LANES], s[:, LANES:]
            if own:
                q0 = (c * RC) % BS
                s0 = jnp.where(kj <= qi + q0, s0, NEG_INF)
                s1 = jnp.where(kj + LANES <= qi + q0, s1, NEG_INF)
            row_max = jnp.max(jnp.maximum(s0, s1), axis=1, keepdims=True)
            if own:
                m_new = jnp.broadcast_to(row_max, (RC, LANES))
            else:
                m_old = m_ref[rs, :]
                m_new = jnp.maximum(m_old, row_max)
            p = jnp.concatenate([jnp.exp(s0 - m_new), jnp.exp(s1 - m_new)], axis=1).astype(bf16)
            pv = jnp.dot(p, v_n, preferred_element_type=f32)
            if own:
                acc_ref[rs, :] = pv
            else:
                acc_ref[rs, :] = jnp.exp(m_old - m_new) * acc_ref[rs, :] + pv
            m_ref[rs, :] = m_new

    def step(n_next, next_ref, n_cur, cur_ref, own):
        for g in range(G):
            scores(g, n_next, next_ref)
            softmax_pv(g, n_cur, cur_ref, own)

    for g in range(G):
        scores(g, i, sa_ref, plain_rows)
    step(0, sb_ref, i, sa_ref, True)

    def body(n, carry):
        @pl.when(n % 2 == 0)
        def _():
            step(n + 1, sa_ref, n, sb_ref, False)

        @pl.when(n % 2 == 1)
        def _():
            step(n + 1, sb_ref, n, sa_ref, False)

        return carry

    lax.fori_loop(0, i, body, 0)
    for jj in range(G * GQA_GROUP // 2):
        a0 = acc_ref[2 * jj * BS:(2 * jj + 1) * BS, :]
        a1 = acc_ref[(2 * jj + 1) * BS:(2 * jj + 2) * BS, :]
        o_ref[:, jj * LANES:(jj + 1) * LANES] = _normalize_pair(a0, a1, 0.0, 0.0).astype(bf16)


def _moba(q, k, v):
    nb = SEQ // MOBA_BLOCK
    G = MOBA_GROUPS
    gw = G * GQA_GROUP * HEAD_DIM
    rows = G * GQA_GROUP * MOBA_BLOCK
    return pl.pallas_call(
        _moba_kernel,
        grid=(BATCH, N_KV_HEADS // G, nb),
        in_specs=[
            pl.BlockSpec((None, G * GQA_GROUP, MOBA_BLOCK, LANES), lambda b, g, i: (b, g, i, 0)),
            pl.BlockSpec((None, G, SEQ, LANES), lambda b, g, i: (b, g, 0, 0)),
            pl.BlockSpec((None, G, SEQ, LANES), lambda b, g, i: (b, g, 0, 0)),
        ],
        out_specs=pl.BlockSpec((None, MOBA_BLOCK, gw), lambda b, g, i: (b, i, g)),
        out_shape=jax.ShapeDtypeStruct((BATCH, SEQ, D_MODEL), bf16),
        scratch_shapes=[
            pltpu.VMEM((G, nb, LANES), f32),
            pltpu.VMEM((rows, LANES), bf16),
            pltpu.VMEM((rows, MOBA_BLOCK), f32),
            pltpu.VMEM((rows, MOBA_BLOCK), f32),
            pltpu.VMEM((rows, LANES), f32),
            pltpu.VMEM((rows, LANES), f32),
        ],
        compiler_params=_params("parallel", "parallel", "arbitrary"),
        name="moba_core",
    )(q, k, v)


def _ret_proj_kernel(x_ref, g_ref, w_ref, cos_ref, sin_ref, o_ref):
    h = _rms_norm(x_ref[...], g_ref[...]).astype(bf16)
    cos = cos_ref[...]
    sin = sin_ref[...]
    half = RET_KEY_DIM // 2
    n_rope = 2 * RET_HEADS
    for c in range(RET_IN_WIDTH // RET_KEY_DIM):
        c0 = c * RET_KEY_DIM
        proj = jnp.dot(h, w_ref[:, c0:c0 + RET_KEY_DIM], preferred_element_type=f32)
        if c < n_rope:
            x1, x2 = proj[:, :half], proj[:, half:]
            r1, r2 = x1 * cos - x2 * sin, x2 * cos + x1 * sin
            if c >= RET_HEADS:
                r1, r2 = r1 * (RET_KEY_DIM ** -0.5), r2 * (RET_KEY_DIM ** -0.5)
            o_ref[:, c0:c0 + half] = r1.astype(bf16)
            o_ref[:, c0 + half:c0 + RET_KEY_DIM] = r2.astype(bf16)
        else:
            o_ref[:, c0:c0 + RET_KEY_DIM] = proj.astype(bf16)


def _ret_proj(x2d, gain, w_bf16, cos, sin):
    nt = TOKENS // PROJ_TM
    return pl.pallas_call(
        _ret_proj_kernel,
        grid=(nt,),
        in_specs=[
            pl.BlockSpec((PROJ_TM, D_MODEL), lambda t: (t, 0)),
            pl.BlockSpec((1, D_MODEL), lambda t: (0, 0)),
            pl.BlockSpec((D_MODEL, RET_IN_WIDTH), lambda t: (0, 0), pipeline_mode=pl.Buffered(1)),
            pl.BlockSpec((PROJ_TM, LANES), lambda t: (t, 0)),
            pl.BlockSpec((PROJ_TM, LANES), lambda t: (t, 0)),
        ],
        out_specs=pl.BlockSpec((PROJ_TM, RET_IN_WIDTH), lambda t: (t, 0)),
        out_shape=jax.ShapeDtypeStruct((TOKENS, RET_IN_WIDTH), bf16),
        compiler_params=_params("parallel"),
        name="ret_proj",
    )(x2d, gain.reshape(1, D_MODEL), w_bf16, cos, sin)


def _ret_core_kernel(cd_ref, q_ref, k_ref, v_ref, g_ref, din_ref, qdec_ref, kdec_ref, gn_ref, y_ref, r_ref):
    t = pl.program_id(1)
    dk, dv = RET_KEY_DIM, RET_VAL_DIM

    @pl.when(t == 0)
    def _():
        r_ref[...] = jnp.zeros_like(r_ref)

    for h in range(RET_HEADS):
        qn = q_ref[:, h * dk:(h + 1) * dk]
        kn = k_ref[:, h * dk:(h + 1) * dk]
        vn = v_ref[:, h * dv:(h + 1) * dv]
        sc = lax.dot_general(qn, kn, NT_DIMS, preferred_element_type=f32) * din_ref[h]
        r_old = r_ref[h]
        qd = (qn.astype(f32) * qdec_ref[h]).astype(bf16)
        o = jnp.dot(jnp.concatenate([sc.astype(bf16), qd], axis=1),
                    jnp.concatenate([vn, r_old.astype(bf16)], axis=0), preferred_element_type=f32)
        kdt = (kn.astype(f32) * kdec_ref[h]).T.astype(bf16)
        r_ref[h] = cd_ref[h] * r_old + jnp.dot(kdt, vn, preferred_element_type=f32)
        mu = jnp.mean(o, axis=-1, keepdims=True)
        d = o - mu
        var = jnp.mean(d * d, axis=-1, keepdims=True)
        on = d * lax.rsqrt(var + EPS) * gn_ref[:, h * dv:(h + 1) * dv]
        y_ref[:, h * dv:(h + 1) * dv] = (_silu(g_ref[:, h * dv:(h + 1) * dv].astype(f32)) * on).astype(bf16)


def _ret_core(proj3d, gn_g, decay_in, q_decay, k_decay, chunk_decay):
    nt = SEQ // RET_TQ
    H, dk, dv, C = RET_HEADS, RET_KEY_DIM, RET_VAL_DIM, RET_TQ
    assert 2 * H * dk == H * dv
    resident = dict(pipeline_mode=pl.Buffered(1))
    return pl.pallas_call(
        _ret_core_kernel,
        grid=(BATCH, nt),
        in_specs=[
            pl.BlockSpec(memory_space=pltpu.SMEM),
            pl.BlockSpec((None, RET_TQ, H * dk), lambda b, t: (b, t, 0)),
            pl.BlockSpec((None, RET_TQ, H * dk), lambda b, t: (b, t, 1)),
            pl.BlockSpec((None, RET_TQ, H * dv), lambda b, t: (b, t, 1)),
            pl.BlockSpec((None, RET_TQ, H * dv), lambda b, t: (b, t, 2)),
            pl.BlockSpec((H, C, C), lambda b, t: (0, 0, 0), **resident),
            pl.BlockSpec((H, C, 1), lambda b, t: (0, 0, 0)),
            pl.BlockSpec((H, C, 1), lambda b, t: (0, 0, 0)),
            pl.BlockSpec((1, H * dv), lambda b, t: (0, 0)),
        ],
        out_specs=pl.BlockSpec((None, RET_TQ, H * dv), lambda b, t: (b, t, 0)),
        out_shape=jax.ShapeDtypeStruct((BATCH, SEQ, H * dv), bf16),
        scratch_shapes=[pltpu.VMEM((H, dk, dv), f32)],
        compiler_params=_params("parallel", "arbitrary"),
        name="ret_core",
    )(chunk_decay, proj3d, proj3d, proj3d, proj3d, decay_in, q_decay, k_decay, gn_g.reshape(1, H * dv))


def _ffn_kernel(y_ref, wo_ref, x_ref, g_ref, wa_ref, wb_ref, cw_ref, cb_ref, wd_ref, fg_ref, o_ref,
                aprev_ref, h_ref, hid_ref, *, final_norm):
    t = pl.program_id(0)

    @pl.when(t == 0)
    def _():
        aprev_ref[...] = jnp.zeros_like(aprev_ref)

    x1 = x_ref[...] + jnp.dot(y_ref[...], wo_ref[...], preferred_element_type=f32)
    o_ref[...] = x1
    h_ref[...] = _rms_norm(x1, g_ref[...]).astype(bf16)
    seq_start = (t * FFN_TM) % SEQ == 0
    for c0 in range(0, D_FF, FFN_CHUNK):
        cs = slice(c0, min(c0 + FFN_CHUNK, D_FF))
        a = jnp.dot(h_ref[...], wa_ref[:, cs], preferred_element_type=f32)
        b = jnp.dot(h_ref[...], wb_ref[:, cs], preferred_element_type=f32)
        prev = jnp.where(seq_start, 0.0, aprev_ref[:, cs])
        aprev_ref[:, cs] = a[FFN_TM - FFN_HALO:, :]
        a_full = jnp.concatenate([prev, a], axis=0)
        a1 = pltpu.roll(a_full, 1, 0)[FFN_HALO:]
        a2 = pltpu.roll(a_full, 2, 0)[FFN_HALO:]
        cw = cw_ref[:, cs]
        conv = cw[0:1] * a2 + cw[1:2] * a1 + cw[2:3] * a + cb_ref[:, cs]
        hid_ref[:, cs] = (_silu(conv) * b).astype(bf16)
    out = o_ref[...] + jnp.dot(hid_ref[...], wd_ref[...], preferred_element_type=f32)
    o_ref[...] = _rms_norm(out, fg_ref[...]) if final_norm else out


def _ffn(y2d, w_out, x2d, gain, wa, wb, conv_w, conv_b, wd, final_gain):
    kdim = y2d.shape[1]
    nt = TOKENS // FFN_TM
    resident = dict(pipeline_mode=pl.Buffered(1))
    final_norm = final_gain is not None
    fg = final_gain if final_norm else gain
    return pl.pallas_call(
        functools.partial(_ffn_kernel, final_norm=final_norm),
        grid=(nt,),
        in_specs=[
            pl.BlockSpec((FFN_TM, kdim), lambda t: (t, 0)),
            pl.BlockSpec((kdim, D_MODEL), lambda t: (0, 0), **resident),
            pl.BlockSpec((FFN_TM, D_MODEL), lambda t: (t, 0)),
            pl.BlockSpec((1, D_MODEL), lambda t: (0, 0)),
            pl.BlockSpec((D_MODEL, D_FF), lambda t: (0, 0), **resident),
            pl.BlockSpec((D_MODEL, D_FF), lambda t: (0, 0), **resident),
            pl.BlockSpec((CONV_WIDTH, D_FF), lambda t: (0, 0)),
            pl.BlockSpec((1, D_FF), lambda t: (0, 0)),
            pl.BlockSpec((D_FF, D_MODEL), lambda t: (0, 0), **resident),
            pl.BlockSpec((1, D_MODEL), lambda t: (0, 0)),
        ],
        out_specs=pl.BlockSpec((FFN_TM, D_MODEL), lambda t: (t, 0)),
        out_shape=jax.ShapeDtypeStruct((TOKENS, D_MODEL), f32),
        scratch_shapes=[
            pltpu.VMEM((FFN_HALO, D_FF), f32),
            pltpu.VMEM((FFN_TM, D_MODEL), bf16),
            pltpu.VMEM((FFN_TM, D_FF), bf16),
        ],
        compiler_params=_params("arbitrary"),
        name="out_proj_conv_ffn",
    )(y2d, w_out, x2d, gain.reshape(1, D_MODEL), wa, wb, conv_w, conv_b.reshape(1, D_FF), wd, fg.reshape(1, D_MODEL))


def _rope_angles(positions, dim):
    inv = 1.0 / (ROPE_THETA ** (jnp.arange(0, dim, 2, dtype=f32) / dim))
    return positions.astype(f32)[..., None] * inv


def _head_rope_tables(positions):
    ang = _rope_angles(positions, HEAD_DIM)
    cos, sin = jnp.cos(ang), jnp.sin(ang)
    reps = LANES // HEAD_DIM
    cosf = jnp.tile(jnp.concatenate([cos, cos], axis=-1), (1, 1, reps))
    sins = jnp.concatenate([jnp.tile(-sin, (1, 1, reps)), jnp.tile(sin, (1, 1, reps))], axis=-1)
    return cosf, sins


def _pair_rope_halves(w_in):
    n_rope = (N_Q_HEADS + N_KV_HEADS) * HEAD_DIM
    wr = w_in[:, :n_rope].reshape(D_MODEL, n_rope // LANES, LANES // HEAD_DIM, 2, QK_HALF)
    wr = wr.transpose(0, 1, 3, 2, 4).reshape(D_MODEL, n_rope)
    return jnp.concatenate([wr, w_in[:, n_rope:]], axis=1)


def _retention_decays():
    H, C = RET_HEADS, RET_TQ
    log_gamma = jnp.log(1.0 - 2.0 ** (-5.0 - jnp.arange(H, dtype=f32)))
    idx = jnp.arange(C, dtype=f32)
    diff = idx[:, None] - idx[None, :]
    decay_in = jnp.where(diff[None] >= 0, jnp.exp(jnp.maximum(diff, 0.0)[None] * log_gamma[:, None, None]), 0.0)
    q_decay = jnp.exp((idx + 1.0)[None, :] * log_gamma[:, None])
    k_decay = jnp.exp((C - 1.0 - idx)[None, :] * log_gamma[:, None])
    chunk_decay = jnp.exp(C * log_gamma)
    return decay_in, q_decay[..., None], k_decay[..., None], chunk_decay


def kernel(x, positions, l0_attn_norm, l0_w_in, l0_w_out, l0_sinks, l0_ffn_norm, l0_w_a, l0_w_b, l0_conv_w, l0_conv_b, l0_w_down, l1_attn_norm, l1_w_in, l1_w_out, l1_gn_g, l1_ffn_norm, l1_w_a, l1_w_b, l1_conv_w, l1_conv_b, l1_w_down, l2_attn_norm, l2_w_in, l2_w_out, l2_ffn_norm, l2_w_a, l2_w_b, l2_conv_w, l2_conv_b, l2_w_down, l3_attn_norm, l3_w_in, l3_w_out, l3_sinks, l3_ffn_norm, l3_w_a, l3_w_b, l3_conv_w, l3_conv_b, l3_w_down, final_norm):
    mixers = [
        (l0_attn_norm, l0_w_in, l0_w_out, l0_sinks),
        (l1_attn_norm, l1_w_in, l1_w_out, l1_gn_g),
        (l2_attn_norm, l2_w_in, l2_w_out, None),
        (l3_attn_norm, l3_w_in, l3_w_out, l3_sinks),
    ]
    ffns = [
        (l0_ffn_norm, l0_w_a, l0_w_b, l0_conv_w, l0_conv_b, l0_w_down),
        (l1_ffn_norm, l1_w_a, l1_w_b, l1_conv_w, l1_conv_b, l1_w_down),
        (l2_ffn_norm, l2_w_a, l2_w_b, l2_conv_w, l2_conv_b, l2_w_down),
        (l3_ffn_norm, l3_w_a, l3_w_b, l3_conv_w, l3_conv_b, l3_w_down),
    ]
    cosf, sins = _head_rope_tables(positions)
    ang_r = _rope_angles(positions, RET_KEY_DIM).reshape(TOKENS, RET_KEY_DIM // 2)
    cos_r, sin_r = jnp.cos(ang_r), jnp.sin(ang_r)
    decay_in, q_decay, k_decay, chunk_decay = _retention_decays()

    x2d = x.reshape(TOKENS, D_MODEL)
    for i in range(DEPTH):
        norm_g, w_in, w_out, extra = mixers[i]
        m = i % N_MIXERS
        w_in_b = w_in.astype(bf16)
        if m == 1:
            proj = _ret_proj(x2d, norm_g, w_in_b, cos_r, sin_r)
            y = _ret_core(proj.reshape(BATCH, SEQ, RET_IN_WIDTH), extra, decay_in, q_decay, k_decay, chunk_decay)
            y2d = y.reshape(TOKENS, RET_HEADS * RET_VAL_DIM)
        else:
            q, k, v = _attn_proj(x2d.reshape(BATCH, SEQ, D_MODEL), norm_g, _pair_rope_halves(w_in_b), cosf, sins)
            y = _swa(q, k, v, extra) if m == 0 else _moba(q, k, v)
            y2d = y.reshape(TOKENS, D_MODEL)
        fg, wa, wb, cw, cb, wd = ffns[i]
        x2d = _ffn(y2d, w_out.astype(bf16), x2d, fg, wa.astype(bf16), wb.astype(bf16), cw, cb, wd.astype(bf16),
                   final_norm if i == DEPTH - 1 else None)
    return x2d.reshape(BATCH, SEQ, D_MODEL)
```

```python
import functools

import jax
import jax.numpy as jnp
from jax import lax
from jax.experimental import pallas as pl
from jax.experimental.pallas import tpu as pltpu

D_MODEL = 1024
BATCH = 4
SEQ = 4096
DEPTH = 4
N_MIXERS = 3
HEAD_DIM = 64
N_Q_HEADS = D_MODEL // HEAD_DIM
N_KV_HEADS = 4
GQA_GROUP = N_Q_HEADS // N_KV_HEADS
QKV_WIDTH = (N_Q_HEADS + 2 * N_KV_HEADS) * HEAD_DIM
ROPE_THETA = 10000.0
SWA_WINDOW = 128
RET_HEADS = 4
RET_KEY_DIM = D_MODEL // RET_HEADS
RET_VAL_DIM = 2 * RET_KEY_DIM
RET_IN_WIDTH = 2 * D_MODEL + 2 * RET_HEADS * RET_VAL_DIM
RET_CHUNK = 128
MOBA_BLOCK = 256
MOBA_TOPK = 3
D_FF = ((8 * D_MODEL // 3 + 127) // 128) * 128
CONV_WIDTH = 3
EPS = 1e-6
NEG_INF = -1e30

LANES = 128
SUBLANES = 8
MXU_WIDTH = 256
VMEM_LIMIT_BYTES = 56 * 1024 * 1024

TOKENS = BATCH * SEQ
PROJ_TM = 512
ATT_TQ = 1024
RET_TQ = 512
MOBA_ROW_CHUNK = 128
MOBA_GROUPS = 4
FFN_TM = 512
FFN_CHUNK = 256
FFN_HALO = SUBLANES

f32 = jnp.float32
bf16 = jnp.bfloat16
NT_DIMS = (((1,), (1,)), ((), ()))


def _params(*semantics):
    return pltpu.CompilerParams(dimension_semantics=semantics, vmem_limit_bytes=VMEM_LIMIT_BYTES)


def _rms_norm(x, gain):
    ms = jnp.mean(x * x, axis=-1, keepdims=True)
    return x * lax.rsqrt(ms + EPS) * gain


def _silu(x):
    half = 0.5 * x
    return half + half * jnp.tanh(half)


def _normalize_pair(pv_even, pv_odd, extra_even, extra_odd):
    low = lax.broadcasted_iota(jnp.int32, pv_even.shape, 1) < HEAD_DIM
    straight = jnp.where(low, pv_even, pv_odd)
    crossed = pltpu.roll(jnp.where(low, pv_odd, pv_even), HEAD_DIM, 1)
    num = jnp.where(low, straight, crossed)
    den = jnp.where(low, crossed + extra_even, straight + extra_odd)
    return num / den


QK_HALF = HEAD_DIM // 2
def _attn_proj_kernel(x_ref, g_ref, w_ref, cos_ref, sin_ref, q_ref, k_ref, v_ref):
    t = pl.program_id(1)
    h = _rms_norm(x_ref[...], g_ref[...]).astype(bf16)
    cosf = cos_ref[...]
    sins = sin_ref[...]
    lane = lax.broadcasted_iota(jnp.int32, (PROJ_TM, LANES), 1)
    low = lane < HEAD_DIM
    qk_lanes = (lane % HEAD_DIM) < QK_HALF
    blk = (t * PROJ_TM + lax.broadcasted_iota(jnp.int32, (PROJ_TM, LANES), 0)) // MOBA_BLOCK
    k_tail = jnp.where(lane == QK_HALF + blk, 1.0, 0.0)

    def split_v(r):
        return jnp.where(low, r, 1.0).astype(bf16), jnp.where(low, pltpu.roll(r, HEAD_DIM, 1), 1.0).astype(bf16)

    def split_qk(r, tail):
        return (jnp.where(qk_lanes, r, tail).astype(bf16),
                jnp.where(qk_lanes, pltpu.roll(r, LANES - QK_HALF, 1), tail).astype(bf16))

    def rope(xc):
        return xc * cosf + pltpu.roll(xc, HEAD_DIM, 1) * sins

    n_q = N_Q_HEADS * HEAD_DIM // LANES
    n_k = N_KV_HEADS * HEAD_DIM // LANES
    groups_per_dot = MXU_WIDTH // LANES
    for d in range(QKV_WIDTH // MXU_WIDTH):
        proj = jnp.dot(h, w_ref[:, d * MXU_WIDTH:(d + 1) * MXU_WIDTH], preferred_element_type=f32)
        for e in range(groups_per_dot):
            c = d * groups_per_dot + e
            xc = proj[:, e * LANES:(e + 1) * LANES]
            if c < n_q:
                q_ref[2 * c], q_ref[2 * c + 1] = split_qk(rope(xc) * (HEAD_DIM ** -0.5), 0.0)
            elif c < n_q + n_k:
                k_ref[2 * (c - n_q)], k_ref[2 * (c - n_q) + 1] = split_qk(rope(xc), k_tail)
            else:
                v_ref[2 * (c - n_q - n_k)], v_ref[2 * (c - n_q - n_k) + 1] = split_v(xc)


def _attn_proj(x, gain, w_bf16, cosf, sins):
    nt = SEQ // PROJ_TM
    return pl.pallas_call(
        _attn_proj_kernel,
        grid=(BATCH, nt),
        in_specs=[
            pl.BlockSpec((None, PROJ_TM, D_MODEL), lambda b, t: (b, t, 0)),
            pl.BlockSpec((1, D_MODEL), lambda b, t: (0, 0)),
            pl.BlockSpec((D_MODEL, QKV_WIDTH), lambda b, t: (0, 0)),
            pl.BlockSpec((None, PROJ_TM, LANES), lambda b, t: (b, t, 0)),
            pl.BlockSpec((None, PROJ_TM, LANES), lambda b, t: (b, t, 0)),
        ],
        out_specs=[
            pl.BlockSpec((None, N_Q_HEADS, PROJ_TM, LANES), lambda b, t: (b, 0, t, 0)),
            pl.BlockSpec((None, N_KV_HEADS, PROJ_TM, LANES), lambda b, t: (b, 0, t, 0)),
            pl.BlockSpec((None, N_KV_HEADS, PROJ_TM, LANES), lambda b, t: (b, 0, t, 0)),
        ],
        out_shape=[
            jax.ShapeDtypeStruct((BATCH, N_Q_HEADS, SEQ, LANES), bf16),
            jax.ShapeDtypeStruct((BATCH, N_KV_HEADS, SEQ, LANES), bf16),
            jax.ShapeDtypeStruct((BATCH, N_KV_HEADS, SEQ, LANES), bf16),
        ],
        compiler_params=_params("parallel", "parallel"),
        name="attn_proj",
    )(x, gain.reshape(1, D_MODEL), w_bf16, cosf, sins)


def _swa_kernel(sinks_ref, q_ref, k_ref, v_ref, o_ref, s_ref):
    g = pl.program_id(1)
    t = pl.program_id(2)
    W = SWA_WINDOW
    nsub = ATT_TQ // W
    qi = lax.broadcasted_iota(jnp.int32, (W, 2 * W), 0)
    col = lax.broadcasted_iota(jnp.int32, (W, 2 * W), 1)
    band = jnp.where((col > qi) & (col <= qi + W), 0.0, NEG_INF)
    first = jnp.where(col <= qi, 0.0, NEG_INF)

    def key_start(c):
        return pl.multiple_of(jnp.maximum(t * ATT_TQ + (c - 1) * W, 0), W)

    def scores(c):
        q4 = q_ref[:, c * W:(c + 1) * W, :].reshape(GQA_GROUP * W, LANES)
        kk = k_ref[pl.ds(key_start(c), 2 * W), :]
        s_ref[c] = lax.dot_general(q4, kk, NT_DIMS, preferred_element_type=f32)

    def head(c, j, vv, bias):
        sink = sinks_ref[g * GQA_GROUP + j]
        s = s_ref[c, j * W:(j + 1) * W, :] + bias
        s0, s1 = s[:, :LANES], s[:, LANES:]
        m = jnp.maximum(jnp.max(jnp.maximum(s0, s1), axis=1, keepdims=True), sink)
        p = jnp.concatenate([jnp.exp(s0 - m), jnp.exp(s1 - m)], axis=1).astype(bf16)
        return jnp.dot(p, vv, preferred_element_type=f32), jnp.exp(sink - m)

    scores(0)
    for c in range(nsub):
        if c + 1 < nsub:
            scores(c + 1)
        vv = v_ref[pl.ds(key_start(c), 2 * W), :]
        bias = jnp.where(t == 0, first, band) if c == 0 else band
        for jj in range(GQA_GROUP // 2):
            pv0, e0 = head(c, 2 * jj, vv, bias)
            pv1, e1 = head(c, 2 * jj + 1, vv, bias)
            o_ref[c * W:(c + 1) * W, jj * LANES:(jj + 1) * LANES] = _normalize_pair(pv0, pv1, e0, e1).astype(bf16)


def _swa(q, k, v, sinks):
    nt = SEQ // ATT_TQ
    gw = GQA_GROUP * HEAD_DIM
    return pl.pallas_call(
        _swa_kernel,
        grid=(BATCH, N_KV_HEADS, nt),
        in_specs=[
            pl.BlockSpec(memory_space=pltpu.SMEM),
            pl.BlockSpec((None, GQA_GROUP, ATT_TQ, LANES), lambda b, g, t: (b, g, t, 0)),
            pl.BlockSpec((None, None, SEQ, LANES), lambda b, g, t: (b, g, 0, 0)),
            pl.BlockSpec((None, None, SEQ, LANES), lambda b, g, t: (b, g, 0, 0)),
        ],
        out_specs=pl.BlockSpec((None, ATT_TQ, gw), lambda b, g, t: (b, t, g)),
        out_shape=jax.ShapeDtypeStruct((BATCH, SEQ, D_MODEL), bf16),
        scratch_shapes=[pltpu.VMEM((ATT_TQ // SWA_WINDOW, GQA_GROUP * SWA_WINDOW, 2 * SWA_WINDOW), f32)],
        compiler_params=_params("parallel", "parallel", "parallel"),
        name="swa_core",
    )(sinks, q, k, v)


def _moba_kernel(q_ref, k_ref, v_ref, o_ref, kmean_ref, qa_ref, sa_ref, sb_ref, m_ref, acc_ref):
    i = pl.program_id(2)
    BS = MOBA_BLOCK
    nb = SEQ // BS
    G = MOBA_GROUPS
    grows = GQA_GROUP * BS
    rows = G * grows

    @pl.when(i == 0)
    def _():
        lane = lax.broadcasted_iota(jnp.int32, (1, LANES), 1)
        for g in range(G):
            for n in range(nb):
                kb = k_ref[g, n * BS:(n + 1) * BS, :].astype(f32)
                kmean_ref[g, n:n + 1, :] = jnp.where(lane % HEAD_DIM < QK_HALF, jnp.mean(kb, axis=0, keepdims=True), 0.0)

    q = q_ref[...].reshape(rows, LANES)
    gate_t = jnp.concatenate(
        [lax.dot_general(kmean_ref[g].astype(bf16), q[g * grows:(g + 1) * grows], NT_DIMS,
                         preferred_element_type=f32) for g in range(G)], axis=1)
    blk = lax.broadcasted_iota(jnp.int32, (nb, rows), 0)
    past = blk < i
    gm = jnp.where(past, gate_t, NEG_INF)
    sel = jnp.zeros((nb, rows), jnp.int32)
    for _ in range(MOBA_TOPK):
        mx = jnp.max(gm, axis=0, keepdims=True)
        cand = (gm == mx) & past & (sel == 0)
        idx = jnp.min(jnp.where(cand, blk, nb), axis=0, keepdims=True)
        pick = blk == idx
        sel = jnp.where(pick, 1, sel)
        gm = jnp.where(pick, NEG_INF, gm)
    bias_t = jnp.where((sel == 1) | jnp.logical_not(past), 0.0, NEG_INF)
    bias_pad_t = jnp.concatenate(
        [jnp.zeros((QK_HALF, rows), f32), bias_t, jnp.zeros((LANES - QK_HALF - nb, rows), f32)], axis=0)
    qa_ref[...] = (q.astype(f32) + bias_pad_t.T).astype(bf16)

    RC = MOBA_ROW_CHUNK
    qi = lax.broadcasted_iota(jnp.int32, (RC, LANES), 0)
    kj = lax.broadcasted_iota(jnp.int32, (RC, LANES), 1)

    def block_start(n):
        return pl.multiple_of(jnp.minimum(n, nb - 1) * BS, BS)

    def plain_rows(g):
        return q_ref[g * GQA_GROUP:(g + 1) * GQA_GROUP].reshape(grows, LANES)

    def biased_rows(g):
        return qa_ref[g * grows:(g + 1) * grows, :]

    def scores(g, n, s_ref, lhs_rows=biased_rows):
        k_n = k_ref[g, pl.ds(block_start(n), BS), :]
        s_ref[g * grows:(g + 1) * grows, :] = lax.dot_general(lhs_rows(g), k_n, NT_DIMS, preferred_element_type=f32)

    def softmax_pv(g, n, s_ref, own):
        v_n = v_ref[g, pl.ds(block_start(n), BS), :]
        for c in range(grows // RC):
            rs = slice(g * grows + c * RC, g * grows + (c + 1) * RC)
            s = s_ref[rs, :]
            s0, s1 = s[:, :LANES], s[:, LANES:]
            if own:
                q0 = (c * RC) % BS
                s0 = jnp.where(kj <= qi + q0, s0, NEG_INF)
                s1 = jnp.where(kj + LANES <= qi + q0, s1, NEG_INF)
            row_max = jnp.max(jnp.maximum(s0, s1), axis=1, keepdims=True)
            if own:
                m_new = jnp.broadcast_to(row_max, (RC, LANES))
            else:
                m_old = m_ref[rs, :]
                m_new = jnp.maximum(m_old, row_max)
            p = jnp.concatenate([jnp.exp(s0 - m_new), jnp.exp(s1 - m_new)], axis=1).astype(bf16)
            pv = jnp.dot(p, v_n, preferred_element_type=f32)
            if own:
                acc_ref[rs, :] = pv
            else:
                acc_ref[rs, :] = jnp.exp(m_old - m_new) * acc_ref[rs, :] + pv
            m_ref[rs, :] = m_new

    def step(n_next, next_ref, n_cur, cur_ref, own):
        for g in range(G):
            scores(g, n_next, next_ref)
            softmax_pv(g, n_cur, cur_ref, own)

    for g in range(G):
        scores(g, i, sa_ref, plain_rows)
    step(0, sb_ref, i, sa_ref, True)

    def body(n, carry):
        @pl.when(n % 2 == 0)
        def _():
            step(n + 1, sa_ref, n, sb_ref, False)

        @pl.when(n % 2 == 1)
        def _():
            step(n + 1, sb_ref, n, sa_ref, False)

        return carry

    lax.fori_loop(0, i, body, 0)
    for jj in range(G * GQA_GROUP // 2):
        a0 = acc_ref[2 * jj * BS:(2 * jj + 1) * BS, :]
        a1 = acc_ref[(2 * jj + 1) * BS:(2 * jj + 2) * BS, :]
        o_ref[:, jj * LANES:(jj + 1) * LANES] = _normalize_pair(a0, a1, 0.0, 0.0).astype(bf16)


def _moba(q, k, v):
    nb = SEQ // MOBA_BLOCK
    G = MOBA_GROUPS
    gw = G * GQA_GROUP * HEAD_DIM
    rows = G * GQA_GROUP * MOBA_BLOCK
    return pl.pallas_call(
        _moba_kernel,
        grid=(BATCH, N_KV_HEADS // G, nb),
        in_specs=[
            pl.BlockSpec((None, G * GQA_GROUP, MOBA_BLOCK, LANES), lambda b, g, i: (b, g, i, 0)),
            pl.BlockSpec((None, G, SEQ, LANES), lambda b, g, i: (b, g, 0, 0)),
            pl.BlockSpec((None, G, SEQ, LANES), lambda b, g, i: (b, g, 0, 0)),
        ],
        out_specs=pl.BlockSpec((None, MOBA_BLOCK, gw), lambda b, g, i: (b, i, g)),
        out_shape=jax.ShapeDtypeStruct((BATCH, SEQ, D_MODEL), bf16),
        scratch_shapes=[
            pltpu.VMEM((G, nb, LANES), f32),
            pltpu.VMEM((rows, LANES), bf16),
            pltpu.VMEM((rows, MOBA_BLOCK), f32),
            pltpu.VMEM((rows, MOBA_BLOCK), f32),
            pltpu.VMEM((rows, LANES), f32),
            pltpu.VMEM((rows, LANES), f32),
        ],
        compiler_params=_params("parallel", "parallel", "arbitrary"),
        name="moba_core",
    )(q, k, v)


def _ret_proj_kernel(x_ref, g_ref, w_ref, cos_ref, sin_ref, o_ref):
    h = _rms_norm(x_ref[...], g_ref[...]).astype(bf16)
    cos = cos_ref[...]
    sin = sin_ref[...]
    half = RET_KEY_DIM // 2
    n_rope = 2 * RET_HEADS
    for c in range(RET_IN_WIDTH // RET_KEY_DIM):
        c0 = c * RET_KEY_DIM
        proj = jnp.dot(h, w_ref[:, c0:c0 + RET_KEY_DIM], preferred_element_type=f32)
        if c < n_rope:
            x1, x2 = proj[:, :half], proj[:, half:]
            r1, r2 = x1 * cos - x2 * sin, x2 * cos + x1 * sin
            if c >= RET_HEADS:
                r1, r2 = r1 * (RET_KEY_DIM ** -0.5), r2 * (RET_KEY_DIM ** -0.5)
            o_ref[:, c0:c0 + half] = r1.astype(bf16)
            o_ref[:, c0 + half:c0 + RET_KEY_DIM] = r2.astype(bf16)
        else:
            o_ref[:, c0:c0 + RET_KEY_DIM] = proj.astype(bf16)


def _ret_proj(x2d, gain, w_bf16, cos, sin):
    nt = TOKENS // PROJ_TM
    return pl.pallas_call(
        _ret_proj_kernel,
        grid=(nt,),
        in_specs=[
            pl.BlockSpec((PROJ_TM, D_MODEL), lambda t: (t, 0)),
            pl.BlockSpec((1, D_MODEL), lambda t: (0, 0)),
            pl.BlockSpec((D_MODEL, RET_IN_WIDTH), lambda t: (0, 0), pipeline_mode=pl.Buffered(1)),
            pl.BlockSpec((PROJ_TM, LANES), lambda t: (t, 0)),
            pl.BlockSpec((PROJ_TM, LANES), lambda t: (t, 0)),
        ],
        out_specs=pl.BlockSpec((PROJ_TM, RET_IN_WIDTH), lambda t: (t, 0)),
        out_shape=jax.ShapeDtypeStruct((TOKENS, RET_IN_WIDTH), bf16),
        compiler_params=_params("parallel"),
        name="ret_proj",
    )(x2d, gain.reshape(1, D_MODEL), w_bf16, cos, sin)


def _ret_core_kernel(cd_ref, q_ref, k_ref, v_ref, g_ref, din_ref, qdec_ref, kdec_ref, gn_ref, y_ref, r_ref):
    t = pl.program_id(1)
    dk, dv = RET_KEY_DIM, RET_VAL_DIM

    @pl.when(t == 0)
    def _():
        r_ref[...] = jnp.zeros_like(r_ref)

    for h in range(RET_HEADS):
        qn = q_ref[:, h * dk:(h + 1) * dk]
        kn = k_ref[:, h * dk:(h + 1) * dk]
        vn = v_ref[:, h * dv:(h + 1) * dv]
        sc = lax.dot_general(qn, kn, NT_DIMS, preferred_element_type=f32) * din_ref[h]
        r_old = r_ref[h]
        qd = (qn.astype(f32) * qdec_ref[h]).astype(bf16)
        o = jnp.dot(jnp.concatenate([sc.astype(bf16), qd], axis=1),
                    jnp.concatenate([vn, r_old.astype(bf16)], axis=0), preferred_element_type=f32)
        kdt = (kn.astype(f32) * kdec_ref[h]).T.astype(bf16)
        r_ref[h] = cd_ref[h] * r_old + jnp.dot(kdt, vn, preferred_element_type=f32)
        mu = jnp.mean(o, axis=-1, keepdims=True)
        d = o - mu
        var = jnp.mean(d * d, axis=-1, keepdims=True)
        on = d * lax.rsqrt(var + EPS) * gn_ref[:, h * dv:(h + 1) * dv]
        y_ref[:, h * dv:(h + 1) * dv] = (_silu(g_ref[:, h * dv:(h + 1) * dv].astype(f32)) * on).astype(bf16)


def _ret_core(proj3d, gn_g, decay_in, q_decay, k_decay, chunk_decay):
    nt = SEQ // RET_TQ
    H, dk, dv, C = RET_HEADS, RET_KEY_DIM, RET_VAL_DIM, RET_TQ
    assert 2 * H * dk == H * dv
    resident = dict(pipeline_mode=pl.Buffered(1))
    return pl.pallas_call(
        _ret_core_kernel,
        grid=(BATCH, nt),
        in_specs=[
            pl.BlockSpec(memory_space=pltpu.SMEM),
            pl.BlockSpec((None, RET_TQ, H * dk), lambda b, t: (b, t, 0)),
            pl.BlockSpec((None, RET_TQ, H * dk), lambda b, t: (b, t, 1)),
            pl.BlockSpec((None, RET_TQ, H * dv), lambda b, t: (b, t, 1)),
            pl.BlockSpec((None, RET_TQ, H * dv), lambda b, t: (b, t, 2)),
            pl.BlockSpec((H, C, C), lambda b, t: (0, 0, 0), **resident),
            pl.BlockSpec((H, C, 1), lambda b, t: (0, 0, 0)),
            pl.BlockSpec((H, C, 1), lambda b, t: (0, 0, 0)),
            pl.BlockSpec((1, H * dv), lambda b, t: (0, 0)),
        ],
        out_specs=pl.BlockSpec((None, RET_TQ, H * dv), lambda b, t: (b, t, 0)),
        out_shape=jax.ShapeDtypeStruct((BATCH, SEQ, H * dv), bf16),
        scratch_shapes=[pltpu.VMEM((H, dk, dv), f32)],
        compiler_params=_params("parallel", "arbitrary"),
        name="ret_core",
    )(chunk_decay, proj3d, proj3d, proj3d, proj3d, decay_in, q_decay, k_decay, gn_g.reshape(1, H * dv))


def _ffn_kernel(y_ref, wo_ref, x_ref, g_ref, wa_ref, wb_ref, cw_ref, cb_ref, wd_ref, fg_ref, o_ref,
                aprev_ref, h_ref, hid_ref, *, final_norm):
    t = pl.program_id(0)

    @pl.when(t == 0)
    def _():
        aprev_ref[...] = jnp.zeros_like(aprev_ref)

    x1 = x_ref[...] + jnp.dot(y_ref[...], wo_ref[...], preferred_element_type=f32)
    o_ref[...] = x1
    h_ref[...] = _rms_norm(x1, g_ref[...]).astype(bf16)
    seq_start = (t * FFN_TM) % SEQ == 0
    for c0 in range(0, D_FF, FFN_CHUNK):
        cs = slice(c0, min(c0 + FFN_CHUNK, D_FF))
        a = jnp.dot(h_ref[...], wa_ref[:, cs], preferred_element_type=f32)
        b = jnp.dot(h_ref[...], wb_ref[:, cs], preferred_element_type=f32)
        prev = jnp.where(seq_start, 0.0, aprev_ref[:, cs])
        aprev_ref[:, cs] = a[FFN_TM - FFN_HALO:, :]
        a_full = jnp.concatenate([prev, a], axis=0)
        a1 = pltpu.roll(a_full, 1, 0)[FFN_HALO:]
        a2 = pltpu.roll(a_full, 2, 0)[FFN_HALO:]
        cw = cw_ref[:, cs]
        conv = cw[0:1] * a2 + cw[1:2] * a1 + cw[2:3] * a + cb_ref[:, cs]
        hid_ref[:, cs] = (_silu(conv) * b).astype(bf16)
    out = o_ref[...] + jnp.dot(hid_ref[...], wd_ref[...], preferred_element_type=f32)
    o_ref[...] = _rms_norm(out, fg_ref[...]) if final_norm else out


def _ffn(y2d, w_out, x2d, gain, wa, wb, conv_w, conv_b, wd, final_gain):
    kdim = y2d.shape[1]
    nt = TOKENS // FFN_TM
    resident = dict(pipeline_mode=pl.Buffered(1))
    final_norm = final_gain is not None
    fg = final_gain if final_norm else gain
    return pl.pallas_call(
        functools.partial(_ffn_kernel, final_norm=final_norm),
        grid=(nt,),
        in_specs=[
            pl.BlockSpec((FFN_TM, kdim), lambda t: (t, 0)),
            pl.BlockSpec((kdim, D_MODEL), lambda t: (0, 0), **resident),
            pl.BlockSpec((FFN_TM, D_MODEL), lambda t: (t, 0)),
            pl.BlockSpec((1, D_MODEL), lambda t: (0, 0)),
            pl.BlockSpec((D_MODEL, D_FF), lambda t: (0, 0), **resident),
            pl.BlockSpec((D_MODEL, D_FF), lambda t: (0, 0), **resident),
            pl.BlockSpec((CONV_WIDTH, D_FF), lambda t: (0, 0)),
            pl.BlockSpec((1, D_FF), lambda t: (0, 0)),
            pl.BlockSpec((D_FF, D_MODEL), lambda t: (0, 0), **resident),
            pl.BlockSpec((1, D_MODEL), lambda t: (0, 0)),
        ],
        out_specs=pl.BlockSpec((FFN_TM, D_MODEL), lambda t: (t, 0)),
        out_shape=jax.ShapeDtypeStruct((TOKENS, D_MODEL), f32),
        scratch_shapes=[
            pltpu.VMEM((FFN_HALO, D_FF), f32),
            pltpu.VMEM((FFN_TM, D_MODEL), bf16),
            pltpu.VMEM((FFN_TM, D_FF), bf16),
        ],
        compiler_params=_params("arbitrary"),
        name="out_proj_conv_ffn",
    )(y2d, w_out, x2d, gain.reshape(1, D_MODEL), wa, wb, conv_w, conv_b.reshape(1, D_FF), wd, fg.reshape(1, D_MODEL))


def _rope_angles(positions, dim):
    inv = 1.0 / (ROPE_THETA ** (jnp.arange(0, dim, 2, dtype=f32) / dim))
    return positions.astype(f32)[..., None] * inv


def _head_rope_tables(positions):
    ang = _rope_angles(positions, HEAD_DIM)
    cos, sin = jnp.cos(ang), jnp.sin(ang)
    reps = LANES // HEAD_DIM
    cosf = jnp.tile(jnp.concatenate([cos, cos], axis=-1), (1, 1, reps))
    sins = jnp.concatenate([jnp.tile(-sin, (1, 1, reps)), jnp.tile(sin, (1, 1, reps))], axis=-1)
    return cosf, sins


def _attn_weight_bf16(w_in):
    n_rope = (N_Q_HEADS + N_KV_HEADS) * HEAD_DIM
    starts = [g0 + head * HEAD_DIM + half * QK_HALF
              for g0 in range(0, n_rope, LANES) for half in range(2) for head in range(LANES // HEAD_DIM)]
    pieces = [w_in[:, c0:c0 + QK_HALF] for c0 in starts] + [w_in[:, n_rope:]]
    return jnp.concatenate(pieces, axis=1).astype(bf16)


def _retention_decays():
    H, C = RET_HEADS, RET_TQ
    log_gamma = jnp.log(1.0 - 2.0 ** (-5.0 - jnp.arange(H, dtype=f32)))
    idx = jnp.arange(C, dtype=f32)
    diff = idx[:, None] - idx[None, :]
    decay_in = jnp.where(diff[None] >= 0, jnp.exp(jnp.maximum(diff, 0.0)[None] * log_gamma[:, None, None]), 0.0)
    q_decay = jnp.exp((idx + 1.0)[None, :] * log_gamma[:, None])
    k_decay = jnp.exp((C - 1.0 - idx)[None, :] * log_gamma[:, None])
    chunk_decay = jnp.exp(C * log_gamma)
    return decay_in, q_decay[..., None], k_decay[..., None], chunk_decay


def kernel(x, positions, l0_attn_norm, l0_w_in, l0_w_out, l0_sinks, l0_ffn_norm, l0_w_a, l0_w_b, l0_conv_w, l0_conv_b, l0_w_down, l1_attn_norm, l1_w_in, l1_w_out, l1_gn_g, l1_ffn_norm, l1_w_a, l1_w_b, l1_conv_w, l1_conv_b, l1_w_down, l2_attn_norm, l2_w_in, l2_w_out, l2_ffn_norm, l2_w_a, l2_w_b, l2_conv_w, l2_conv_b, l2_w_down, l3_attn_norm, l3_w_in, l3_w_out, l3_sinks, l3_ffn_norm, l3_w_a, l3_w_b, l3_conv_w, l3_conv_b, l3_w_down, final_norm):
    mixers = [
        (l0_attn_norm, l0_w_in, l0_w_out, l0_sinks),
        (l1_attn_norm, l1_w_in, l1_w_out, l1_gn_g),
        (l2_attn_norm, l2_w_in, l2_w_out, None),
        (l3_attn_norm, l3_w_in, l3_w_out, l3_sinks),
    ]
    ffns = [
        (l0_ffn_norm, l0_w_a, l0_w_b, l0_conv_w, l0_conv_b, l0_w_down),
        (l1_ffn_norm, l1_w_a, l1_w_b, l1_conv_w, l1_conv_b, l1_w_down),
        (l2_ffn_norm, l2_w_a, l2_w_b, l2_conv_w, l2_conv_b, l2_w_down),
        (l3_ffn_norm, l3_w_a, l3_w_b, l3_conv_w, l3_conv_b, l3_w_down),
    ]
    cosf, sins = _head_rope_tables(positions)
    ang_r = _rope_angles(positions, RET_KEY_DIM).reshape(TOKENS, RET_KEY_DIM // 2)
    cos_r, sin_r = jnp.cos(ang_r), jnp.sin(ang_r)
    decay_in, q_decay, k_decay, chunk_decay = _retention_decays()

    x2d = x.reshape(TOKENS, D_MODEL)
    for i in range(DEPTH):
        norm_g, w_in, w_out, extra = mixers[i]
        m = i % N_MIXERS
        if m == 1:
            proj = _ret_proj(x2d, norm_g, w_in.astype(bf16), cos_r, sin_r)
            y = _ret_core(proj.reshape(BATCH, SEQ, RET_IN_WIDTH), extra, decay_in, q_decay, k_decay, chunk_decay)
            y2d = y.reshape(TOKENS, RET_HEADS * RET_VAL_DIM)
        else:
            q, k, v = _attn_proj(x2d.reshape(BATCH, SEQ, D_MODEL), norm_g, _attn_weight_bf16(w_in), cosf, sins)
            y = _swa(q, k, v, extra) if m == 0 else _moba(q, k, v)
            y2d = y.reshape(TOKENS, D_MODEL)
        fg, wa, wb, cw, cb, wd = ffns[i]
        x2d = _ffn(y2d, w_out.astype(bf16), x2d, fg, wa.astype(bf16), wb.astype(bf16), cw, cb, wd.astype(bf16),
                   final_norm if i == DEPTH - 1 else None)
    return x2d.reshape(BATCH, SEQ, D_MODEL)
```
